```python
import jax, jax.numpy as jnp
from jax import lax
import numpy as np

D_MODEL = 1024
BATCH = 8
SEQ = 4096
DEPTH = 4

N_A_LAYERS = DEPTH // 2
N_B_LAYERS = DEPTH - N_A_LAYERS
CONV_WIDTH = 3
D_FF = 2816
N_HEADS = 8
QK_NOPE_DIM = 128
QK_ROPE_DIM = 64
V_HEAD_DIM = 128
Q_LORA_RANK = 512
KV_LORA_RANK = 256
ROPE_THETA = 10000.0
RMS_EPS = 1e-6
Q_BLOCK = 128

kernel_name = "yoco_shortconv_mla_convffn_trunk"


def rms_norm(x, g):
    xf = x.astype(jnp.float32)
    y = xf * lax.rsqrt(jnp.mean(xf * xf, axis=-1, keepdims=True) + RMS_EPS)
    return (y * g.astype(jnp.float32)).astype(x.dtype)


def causal_dwconv(x, w):
    c = x.shape[-1]
    return lax.conv_general_dilated(
        x, w[:, None, :].astype(x.dtype), window_strides=(1,),
        padding=((CONV_WIDTH - 1, 0),), dimension_numbers=("NWC", "WIO", "NWC"),
        feature_group_count=c)


def rope_tables(seq, dtype):
    inv = 1.0 / (ROPE_THETA ** (jnp.arange(0, QK_ROPE_DIM, 2, dtype=jnp.float32) / QK_ROPE_DIM))
    ang = jnp.arange(seq, dtype=jnp.float32)[:, None] * inv[None, :]
    return jnp.cos(ang).astype(dtype), jnp.sin(ang).astype(dtype)


def apply_rope(x, cos, sin):
    x1, x2 = jnp.split(x, 2, axis=-1)
    return jnp.concatenate([x1 * cos - x2 * sin, x1 * sin + x2 * cos], axis=-1)


def short_conv_mixer(x_n, w_in, conv_w, w_out):
    b, c, h = jnp.split(x_n @ w_in, 3, axis=-1)
    return (b * causal_dwconv(c * h, conv_w)) @ w_out


def conv_ffn(x_n, w_up, conv_w, w_down):
    g, u = jnp.split(causal_dwconv(x_n @ w_up, conv_w), 2, axis=-1)
    return (jax.nn.silu(g) * u) @ w_down


def shared_latent_kv(h, kv_in_norm, w_dkv, kv_norm, w_ukv, cos, sin):
    bsz, seq, _ = h.shape
    ckv = rms_norm(h, kv_in_norm) @ w_dkv
    c_kv, k_rope = ckv[..., :KV_LORA_RANK], ckv[..., KV_LORA_RANK:]
    k_rope = apply_rope(k_rope, cos, sin)
    kv = (rms_norm(c_kv, kv_norm) @ w_ukv).reshape(bsz, seq, N_HEADS, QK_NOPE_DIM + V_HEAD_DIM)
    return kv[..., :QK_NOPE_DIM], kv[..., QK_NOPE_DIM:], k_rope


def mla_attention(x_n, w_dq, q_norm, w_uq, w_o, k_nope, v, k_rope, cos, sin):
    bsz, seq, _ = x_n.shape
    q = (rms_norm(x_n @ w_dq, q_norm) @ w_uq).reshape(bsz, seq, N_HEADS, QK_NOPE_DIM + QK_ROPE_DIM)
    q_nope = q[..., :QK_NOPE_DIM]
    q_rope = apply_rope(q[..., QK_NOPE_DIM:], cos[:, None, :], sin[:, None, :])
    n_blk = seq // Q_BLOCK

    def to_blocks(t):
        return jnp.moveaxis(t.reshape(bsz, n_blk, Q_BLOCK, *t.shape[2:]), 1, 0)

    k_pos = jnp.arange(seq)
    scale = (QK_NOPE_DIM + QK_ROPE_DIM) ** -0.5

    def attend_block(args):
        qn, qr, blk = args
        s = (jnp.einsum('bqhd,bkhd->bhqk', qn, k_nope)
             + jnp.einsum('bqhr,bkr->bhqk', qr, k_rope)).astype(jnp.float32) * scale
        q_pos = blk * Q_BLOCK + jnp.arange(Q_BLOCK)
        s = jnp.where(k_pos[None, :] <= q_pos[:, None], s, -jnp.inf)
        p = jax.nn.softmax(s, axis=-1).astype(v.dtype)
        return jnp.einsum('bhqk,bkhd->bqhd', p, v)

    o = lax.map(attend_block, (to_blocks(q_nope), to_blocks(q_rope), jnp.arange(n_blk)))
    o = jnp.moveaxis(o, 0, 1).reshape(bsz, seq, N_HEADS * V_HEAD_DIM)
    return o @ w_o


def setup_inputs(seed: int = 0) -> dict:
    key = jax.random.key(seed)
    ks = jax.random.split(key, 24)
    f32 = jnp.float32

    def w(k, shape, fan_in):
        return jax.random.normal(k, shape, f32) * (fan_in ** -0.5)

    def gain(k, shape):
        return 1.0 + 0.02 * jax.random.normal(k, shape, f32)

    return {
        "x": jax.random.normal(ks[0], (BATCH, SEQ, D_MODEL), f32),
        "a_mix_norm": gain(ks[1], (N_A_LAYERS, D_MODEL)),
        "a_w_in": w(ks[2], (N_A_LAYERS, D_MODEL, 3 * D_MODEL), D_MODEL),
        "a_conv": w(ks[3], (N_A_LAYERS, CONV_WIDTH, D_MODEL), CONV_WIDTH),
        "a_w_out": w(ks[4], (N_A_LAYERS, D_MODEL, D_MODEL), D_MODEL),
        "b_mix_norm": gain(ks[5], (N_B_LAYERS, D_MODEL)),
        "b_w_dq": w(ks[6], (N_B_LAYERS, D_MODEL, Q_LORA_RANK), D_MODEL),
        "b_q_norm": gain(ks[7], (N_B_LAYERS, Q_LORA_RANK)),
        "b_w_uq": w(ks[8], (N_B_LAYERS, Q_LORA_RANK, N_HEADS * (QK_NOPE_DIM + QK_ROPE_DIM)), Q_LORA_RANK),
        "b_w_o": w(ks[9], (N_B_LAYERS, N_HEADS * V_HEAD_DIM, D_MODEL), N_HEADS * V_HEAD_DIM),
        "kv_in_norm": gain(ks[10], (D_MODEL,)),
        "w_dkv": w(ks[11], (D_MODEL, KV_LORA_RANK + QK_ROPE_DIM), D_MODEL),
        "kv_norm": gain(ks[12], (KV_LORA_RANK,)),
        "w_ukv": w(ks[13], (KV_LORA_RANK, N_HEADS * (QK_NOPE_DIM + V_HEAD_DIM)), KV_LORA_RANK),
        "ffn_norm": gain(ks[14], (DEPTH, D_MODEL)),
        "ffn_w_up": w(ks[15], (DEPTH, D_MODEL, 2 * D_FF), D_MODEL),
        "ffn_conv": w(ks[16], (DEPTH, CONV_WIDTH, 2 * D_FF), CONV_WIDTH),
        "ffn_w_down": w(ks[17], (DEPTH, D_FF, D_MODEL), D_FF),
        "final_norm": gain(ks[18], (D_MODEL,)),
    }


def reference(x, a_mix_norm, a_w_in, a_conv, a_w_out, b_mix_norm, b_w_dq, b_q_norm, b_w_uq, b_w_o,
              kv_in_norm, w_dkv, kv_norm, w_ukv, ffn_norm, ffn_w_up, ffn_conv, ffn_w_down, final_norm):
    cos, sin = rope_tables(x.shape[1], x.dtype)
    h = x
    shared = None
    for layer in range(DEPTH):
        if layer < N_A_LAYERS:
            h = h + short_conv_mixer(rms_norm(h, a_mix_norm[layer]), a_w_in[layer],
                                     a_conv[layer], a_w_out[layer])
        else:
            j = layer - N_A_LAYERS
            k_nope, v, k_rope = shared
            h = h + mla_attention(rms_norm(h, b_mix_norm[j]), b_w_dq[j], b_q_norm[j], b_w_uq[j],
                                  b_w_o[j], k_nope, v, k_rope, cos, sin)
        h = h + conv_ffn(rms_norm(h, ffn_norm[layer]), ffn_w_up[layer], ffn_conv[layer], ffn_w_down[layer])
        if layer == N_A_LAYERS - 1:
            shared = shared_latent_kv(h, kv_in_norm, w_dkv, kv_norm, w_ukv, cos, sin)
    return rms_norm(h, final_norm)
```

```python
import functools
import math

import jax
import jax.numpy as jnp
from jax import lax
from jax.experimental import pallas as pl
from jax.experimental.pallas import tpu as pltpu

D_MODEL = 1024
N_A_LAYERS = 2
N_B_LAYERS = 2
CONV_WIDTH = 3
D_FF = 2816
N_HEADS = 8
QK_NOPE_DIM = 128
QK_ROPE_DIM = 64
V_HEAD_DIM = 128
Q_LORA_RANK = 512
KV_LORA_RANK = 256
ROPE_THETA = 10000.0
RMS_EPS = 1e-6

LANES = 128
SUBLANES = 8
MXU_DIM = 256
VMEM_LIMIT_BYTES = 56 * 1024 * 1024

SEQ_TILE = 512
HALO = SUBLANES
A_CHUNK = MXU_DIM
A_NCHUNK = D_MODEL // A_CHUNK
F_CHUNK = MXU_DIM
F_NCHUNK = D_FF // F_CHUNK
HEAD_W = QK_NOPE_DIM + 2 * QK_ROPE_DIM
Q_TILE = 512
KV_TILE = 512
MASK_VALUE = -1e30

BF16 = jnp.bfloat16
F32 = jnp.float32


def _rms(x, g):
    ms = jnp.mean(x * x, axis=-1, keepdims=True)
    return x * lax.rsqrt(ms + RMS_EPS) * g


def _dot(a, b):
    return jnp.dot(a, b, preferred_element_type=F32)


def _causal_conv3(buf_ref, cw, rows):
    x0 = buf_ref[pl.ds(HALO, rows), :]
    x1 = buf_ref[pl.ds(HALO - 1, rows), :]
    x2 = buf_ref[pl.ds(HALO - 2, rows), :]
    return cw[2:3, :] * x0 + cw[1:2, :] * x1 + cw[0:1, :] * x2


def _const_spec(shape):
    n = len(shape)
    return pl.BlockSpec(shape, lambda *_: (0,) * n, pipeline_mode=pl.Buffered(1))


def _params():
    return pltpu.CompilerParams(
        dimension_semantics=("arbitrary", "arbitrary"),
        vmem_limit_bytes=VMEM_LIMIT_BYTES)


def _mixer_kernel(h_ref, g_ref, win_ref, cw_ref, wout_ref, o_ref,
                  xn_sc, z_sc, ch_sc, halo_sc, m_sc):
    ts = h_ref.shape[0]

    @pl.when(pl.program_id(1) == 0)
    def _():
        halo_sc[...] = jnp.zeros_like(halo_sc)

    xn_sc[...] = _rms(h_ref[...], g_ref[...]).astype(BF16)
    for k in range(A_NCHUNK):
        z_sc[...] = _dot(xn_sc[...], win_ref[k])
        ch_sc[pl.ds(0, HALO), :] = halo_sc[k]
        ch_sc[pl.ds(HALO, ts), :] = z_sc[:, A_CHUNK:2 * A_CHUNK] * z_sc[:, 2 * A_CHUNK:]
        y = _causal_conv3(ch_sc, cw_ref[k], ts)
        halo_sc[k] = ch_sc[pl.ds(ts, HALO), :]
        m_sc[k] = (z_sc[:, :A_CHUNK] * y).astype(BF16)
    acc = _dot(m_sc[0], wout_ref[0])
    for k in range(1, A_NCHUNK):
        acc = acc + _dot(m_sc[k], wout_ref[k])
    o_ref[...] = h_ref[...] + acc


def _mixer(h, g, win_r, cw_r, wout_r):
    b, s, d = h.shape
    ts = SEQ_TILE
    tok = pl.BlockSpec((None, ts, d), lambda i, j: (i, j, 0))
    return pl.pallas_call(
        _mixer_kernel,
        grid=(b, s // ts),
        in_specs=[tok, _const_spec(g.shape), _const_spec(win_r.shape),
                  _const_spec(cw_r.shape), _const_spec(wout_r.shape)],
        out_specs=tok,
        out_shape=jax.ShapeDtypeStruct(h.shape, F32),
        scratch_shapes=[
            pltpu.VMEM((ts, d), BF16),
            pltpu.VMEM((ts, 3 * A_CHUNK), F32),
            pltpu.VMEM((ts + HALO, A_CHUNK), F32),
            pltpu.VMEM((A_NCHUNK, HALO, A_CHUNK), F32),
            pltpu.VMEM((A_NCHUNK, ts, A_CHUNK), BF16),
        ],
        compiler_params=_params(),
        name="short_conv_mixer",
    )(h, g, win_r, cw_r, wout_r)


def _ffn_kernel(*refs, has_oproj, has_final):
    refs = list(refs)
    h_ref = refs.pop(0)
    if has_oproj:
        o_ref, wo_ref = refs.pop(0), refs.pop(0)
    g_ref, wup_ref, cw_ref, wdn_ref = refs[:4]
    refs = refs[4:]
    if has_final:
        gf_ref = refs.pop(0)
    out_ref, xn_sc, up_sc, halo_sc, act_sc = refs
    ts = h_ref.shape[0]

    @pl.when(pl.program_id(1) == 0)
    def _():
        halo_sc[...] = jnp.zeros_like(halo_sc)

    h = h_ref[...]
    if has_oproj:
        h = h + _dot(o_ref[...], wo_ref[...])
    out_ref[...] = h
    xn_sc[...] = _rms(h, g_ref[...]).astype(BF16)

    def chunk(k, carry):
        up_sc[pl.ds(0, HALO), :] = halo_sc[k]
        up_sc[pl.ds(HALO, ts), :] = _dot(xn_sc[...], wup_ref[k])
        y = _causal_conv3(up_sc, cw_ref[k], ts)
        halo_sc[k] = up_sc[pl.ds(ts, HALO), :]
        gate = y[:, :F_CHUNK]
        act_sc[k] = (gate / (1.0 + jnp.exp(-gate)) * y[:, F_CHUNK:]).astype(BF16)
        return carry

    lax.fori_loop(0, F_NCHUNK, chunk, 0)

    acc = _dot(act_sc[0], wdn_ref[0])
    for k in range(1, F_NCHUNK):
        acc = acc + _dot(act_sc[k], wdn_ref[k])
    out = out_ref[...] + acc
    if has_final:
        out = _rms(out, gf_ref[...])
    out_ref[...] = out


def _ffn(h, g, wup_r, cw_r, wdn_r, *, o=None, wo=None, gf=None):
    b, s, d = h.shape
    ts = SEQ_TILE
    tok = pl.BlockSpec((None, ts, d), lambda i, j: (i, j, 0))
    args, specs = [h], [tok]
    if o is not None:
        args += [o, wo]
        specs += [tok, _const_spec(wo.shape)]
    args += [g, wup_r, cw_r, wdn_r]
    specs += [_const_spec(g.shape), _const_spec(wup_r.shape), _const_spec(cw_r.shape),
              _const_spec(wdn_r.shape)]
    if gf is not None:
        args.append(gf)
        specs.append(_const_spec(gf.shape))
    return pl.pallas_call(
        functools.partial(_ffn_kernel, has_oproj=o is not None, has_final=gf is not None),
        grid=(b, s // ts),
        in_specs=specs,
        out_specs=tok,
        out_shape=jax.ShapeDtypeStruct(h.shape, F32),
        scratch_shapes=[
            pltpu.VMEM((ts, d), BF16),
            pltpu.VMEM((ts + HALO, 2 * F_CHUNK), F32),
            pltpu.VMEM((F_NCHUNK, HALO, 2 * F_CHUNK), F32),
            pltpu.VMEM((F_NCHUNK, ts, F_CHUNK), BF16),
        ],
        compiler_params=_params(),
        name="conv_ffn",
    )(*args)


def _kv_kernel(h_ref, gin_ref, wdkv_ref, gkv_ref, wukv_ref, rope_ref, k_ref, v_ref):
    xn = _rms(h_ref[...], gin_ref[...]).astype(BF16)
    ckv = _dot(xn, wdkv_ref[...])
    cn = _rms(ckv[:, :KV_LORA_RANK], gkv_ref[...]).astype(BF16)
    kv = _dot(cn, wukv_ref[...])
    r = ckv[:, KV_LORA_RANK:] * rope_ref[...]
    kr = (r + pltpu.roll(r, QK_ROPE_DIM, axis=1)).astype(BF16)
    for hh in range(N_HEADS):
        k_ref[hh, :, :QK_NOPE_DIM] = kv[:, hh * QK_NOPE_DIM:(hh + 1) * QK_NOPE_DIM].astype(BF16)
        k_ref[hh, :, QK_NOPE_DIM:] = kr
        off = N_HEADS * QK_NOPE_DIM + hh * V_HEAD_DIM
        v_ref[hh] = kv[:, off:off + V_HEAD_DIM].astype(BF16)


def _shared_kv(h, gin, wdkv_x, gkv, wukv_p, rope_t):
    b, s, d = h.shape
    ts = SEQ_TILE
    return pl.pallas_call(
        _kv_kernel,
        grid=(b, s // ts),
        in_specs=[pl.BlockSpec((None, ts, d), lambda i, j: (i, j, 0)),
                  _const_spec(gin.shape), _const_spec(wdkv_x.shape), _const_spec(gkv.shape),
                  _const_spec(wukv_p.shape),
                  pl.BlockSpec((ts, LANES), lambda i, j: (j, 0))],
        out_specs=[pl.BlockSpec((None, N_HEADS, ts, HEAD_W), lambda i, j: (i, 0, j, 0)),
                   pl.BlockSpec((None, N_HEADS, ts, V_HEAD_DIM), lambda i, j: (i, 0, j, 0))],
        out_shape=[jax.ShapeDtypeStruct((b, N_HEADS, s, HEAD_W), BF16),
                   jax.ShapeDtypeStruct((b, N_HEADS, s, V_HEAD_DIM), BF16)],
        compiler_params=_params(),
        name="shared_latent_kv",
    )(h, gin, wdkv_x, gkv, wukv_p, rope_t)


def _q_kernel(h_ref, g_ref, wdq_ref, gq_ref, wuq_ref, rope_ref, q_ref, *, qscale):
    xn = _rms(h_ref[...], g_ref[...]).astype(BF16)
    qn = _rms(_dot(xn, wdq_ref[...]), gq_ref[...]).astype(BF16)
    q = _dot(qn, wuq_ref[...])
    rope = rope_ref[...] * qscale
    for hh in range(N_HEADS):
        base = hh * HEAD_W
        q_ref[hh, :, :QK_NOPE_DIM] = (q[:, base:base + QK_NOPE_DIM] * qscale).astype(BF16)
        q_ref[hh, :, QK_NOPE_DIM:] = (q[:, base + QK_NOPE_DIM:base + HEAD_W] * rope).astype(BF16)


def _q_proj(h, g, wdq, gq, wuq_p, rope_t, qscale):
    b, s, d = h.shape
    ts = SEQ_TILE
    return pl.pallas_call(
        functools.partial(_q_kernel, qscale=qscale),
        grid=(b, s // ts),
        in_specs=[pl.BlockSpec((None, ts, d), lambda i, j: (i, j, 0)),
                  _const_spec(g.shape), _const_spec(wdq.shape), _const_spec(gq.shape),
                  _const_spec(wuq_p.shape),
                  pl.BlockSpec((ts, LANES), lambda i, j: (j, 0))],
        out_specs=pl.BlockSpec((None, N_HEADS, ts, HEAD_W), lambda i, j: (i, 0, j, 0)),
        out_shape=jax.ShapeDtypeStruct((b, N_HEADS, s, HEAD_W), BF16),
        compiler_params=_params(),
        name="mla_q_proj",
    )(h, g, wdq, gq, wuq_p, rope_t)


def _attn_kernel(q_ref, k_ref, v_ref, o_ref, m_sc, l_sc, acc_sc):
    qi = pl.program_id(2)
    tq = q_ref.shape[0]
    tk = KV_TILE
    q = q_ref[...]

    def scores(j):
        start = pl.multiple_of(j * tk, tk)
        k = k_ref[pl.ds(start, tk), :]
        v = v_ref[pl.ds(start, tk), :]
        s = lax.dot_general(q, k, (((1,), (1,)), ((), ())), preferred_element_type=F32)
        return s, v

    s, v = scores(qi)
    row = lax.broadcasted_iota(jnp.int32, (tq, tk), 0)
    col = lax.broadcasted_iota(jnp.int32, (tq, tk), 1)
    s = jnp.where(col <= row, s, MASK_VALUE)
    m = jnp.max(s, axis=-1, keepdims=True)
    p = jnp.exp2(s - m)
    m_sc[...] = jnp.broadcast_to(m, m_sc.shape)
    l_sc[...] = jnp.broadcast_to(jnp.sum(p, axis=-1, keepdims=True), l_sc.shape)
    acc_sc[...] = _dot(p.astype(BF16), v)

    def body(j, carry):
        s, v = scores(j)
        m_old = m_sc[...]
        m_new = jnp.maximum(m_old, jnp.max(s, axis=-1, keepdims=True))
        alpha = jnp.exp2(m_old - m_new)
        p = jnp.exp2(s - m_new[:, :1])
        l_sc[...] = alpha * l_sc[...] + jnp.sum(p, axis=-1, keepdims=True)
        acc_sc[...] = alpha * acc_sc[...] + _dot(p.astype(BF16), v)
        m_sc[...] = m_new
        return carry

    lax.fori_loop(0, qi, body, 0)
    o_ref[...] = (acc_sc[...] / l_sc[...]).astype(o_ref.dtype)


def _attention(q, k, v):
    b, nh, s, w = q.shape
    tq = Q_TILE
    return pl.pallas_call(
        _attn_kernel,
        grid=(b, nh, s // tq),
        in_specs=[pl.BlockSpec((None, None, tq, w), lambda i, h, j: (i, h, j, 0)),
                  pl.BlockSpec((None, None, s, w), lambda i, h, j: (i, h, 0, 0)),
                  pl.BlockSpec((None, None, s, V_HEAD_DIM), lambda i, h, j: (i, h, 0, 0))],
        out_specs=pl.BlockSpec((None, tq, V_HEAD_DIM), lambda i, h, j: (i, j, h)),
        out_shape=jax.ShapeDtypeStruct((b, s, nh * V_HEAD_DIM), BF16),
        scratch_shapes=[pltpu.VMEM((tq, LANES), F32), pltpu.VMEM((tq, LANES), F32),
                        pltpu.VMEM((tq, V_HEAD_DIM), F32)],
        compiler_params=pltpu.CompilerParams(
            dimension_semantics=("arbitrary", "arbitrary", "arbitrary"),
            vmem_limit_bytes=VMEM_LIMIT_BYTES),
        name="mla_flash_attention",
    )(q, k, v)


def _chunk_cols(w, parts, chunk):
    k, n = w.shape
    c = n // parts
    w = w.reshape(k, parts, c // chunk, chunk)
    return jnp.transpose(w, (2, 0, 1, 3)).reshape(c // chunk, k, parts * chunk)


def _rope_table(seq):
    inv = 1.0 / (ROPE_THETA ** (jnp.arange(0, QK_ROPE_DIM, 2, dtype=F32) / QK_ROPE_DIM))
    ang = jnp.arange(seq, dtype=F32)[:, None] * inv[None, :]
    cos, sin = jnp.cos(ang), jnp.sin(ang)
    return jnp.concatenate([cos, cos, -sin, sin], axis=-1)


def _swap_halves(w):
    half = w.shape[-1] // 2
    return jnp.concatenate([w[..., half:], w[..., :half]], axis=-1)


def kernel(x, a_mix_norm, a_w_in, a_conv, a_w_out, b_mix_norm, b_w_dq, b_q_norm, b_w_uq, b_w_o,
           kv_in_norm, w_dkv, kv_norm, w_ukv, ffn_norm, ffn_w_up, ffn_conv, ffn_w_down, final_norm):
    seq = x.shape[1]
    rope_t = _rope_table(seq)
    qscale = (QK_NOPE_DIM + QK_ROPE_DIM) ** -0.5 * math.log2(math.e)

    def row(v):
        return v.reshape(1, -1)

    h = x
    kk = vv = None
    for layer in range(N_A_LAYERS + N_B_LAYERS):
        o = wo = None
        if layer < N_A_LAYERS:
            win_r = _chunk_cols(a_w_in[layer].astype(BF16), 3, A_CHUNK)
            cw_r = jnp.transpose(a_conv[layer].reshape(CONV_WIDTH, A_NCHUNK, A_CHUNK), (1, 0, 2))
            wout_r = a_w_out[layer].astype(BF16).reshape(A_NCHUNK, A_CHUNK, D_MODEL)
            h = _mixer(h, row(a_mix_norm[layer]), win_r, cw_r, wout_r)
        else:
            j = layer - N_A_LAYERS
            wuq = b_w_uq[j].reshape(Q_LORA_RANK, N_HEADS, QK_NOPE_DIM + QK_ROPE_DIM)
            wuq_rope = wuq[..., QK_NOPE_DIM:]
            wuq_p = jnp.concatenate([wuq[..., :QK_NOPE_DIM], wuq_rope, _swap_halves(wuq_rope)],
                                    axis=-1).reshape(Q_LORA_RANK, N_HEADS * HEAD_W).astype(BF16)
            q = _q_proj(h, row(b_mix_norm[j]), b_w_dq[j].astype(BF16), row(b_q_norm[j]), wuq_p,
                        rope_t, qscale)
            o = _attention(q, kk, vv)
            wo = b_w_o[j].astype(BF16)
        wup_r = _chunk_cols(ffn_w_up[layer].astype(BF16), 2, F_CHUNK)
        fcw_r = jnp.transpose(
            ffn_conv[layer].reshape(CONV_WIDTH, 2, F_NCHUNK, F_CHUNK), (2, 0, 1, 3)
        ).reshape(F_NCHUNK, CONV_WIDTH, 2 * F_CHUNK)
        wdn_r = ffn_w_down[layer].astype(BF16).reshape(F_NCHUNK, F_CHUNK, D_MODEL)
        last = layer == N_A_LAYERS + N_B_LAYERS - 1
        h = _ffn(h, row(ffn_norm[layer]), wup_r, fcw_r, wdn_r, o=o, wo=wo,
                 gf=row(final_norm) if last else None)
        if layer == N_A_LAYERS - 1:
            k_rope_w = w_dkv[:, KV_LORA_RANK:]
            wdkv_x = jnp.concatenate([w_dkv, _swap_halves(k_rope_w)], axis=-1).astype(BF16)
            wukv = w_ukv.reshape(KV_LORA_RANK, N_HEADS, QK_NOPE_DIM + V_HEAD_DIM)
            wukv_p = jnp.concatenate(
                [wukv[..., :QK_NOPE_DIM].reshape(KV_LORA_RANK, -1),
                 wukv[..., QK_NOPE_DIM:].reshape(KV_LORA_RANK, -1)], axis=-1).astype(BF16)
            kk, vv = _shared_kv(h, row(kv_in_norm), wdkv_x, row(kv_norm), wukv_p, rope_t)
    return h
```

```python
import functools
import math

import jax
import jax.numpy as jnp
from jax import lax
from jax.experimental import pallas as pl
from jax.experimental.pallas import tpu as pltpu

D_MODEL = 1024
N_A_LAYERS = 2
N_B_LAYERS = 2
CONV_WIDTH = 3
D_FF = 2816
N_HEADS = 8
QK_NOPE_DIM = 128
QK_ROPE_DIM = 64
V_HEAD_DIM = 128
Q_LORA_RANK = 512
KV_LORA_RANK = 256
ROPE_THETA = 10000.0
RMS_EPS = 1e-6

LANES = 128
SUBLANES = 8
MXU_DIM = 256
VMEM_LIMIT_BYTES = 56 * 1024 * 1024

SEQ_TILE = 512
HALO = SUBLANES
A_CHUNK = MXU_DIM
A_NCHUNK = D_MODEL // A_CHUNK
F_CHUNK = MXU_DIM
F_NCHUNK = D_FF // F_CHUNK
HEAD_W = QK_NOPE_DIM + 2 * QK_ROPE_DIM
Q_TILE = 512
KV_TILE = 512
MASK_VALUE = -1e30

BF16 = jnp.bfloat16
F32 = jnp.float32


def _rms(x, g):
    ms = jnp.mean(x * x, axis=-1, keepdims=True)
    return x * lax.rsqrt(ms + RMS_EPS) * g


def _dot(a, b):
    return jnp.dot(a, b, preferred_element_type=F32)


def _causal_conv3(buf_ref, cw, rows):
    x0 = buf_ref[pl.ds(HALO, rows), :]
    x1 = buf_ref[pl.ds(HALO - 1, rows), :]
    x2 = buf_ref[pl.ds(HALO - 2, rows), :]
    return cw[2:3, :] * x0 + cw[1:2, :] * x1 + cw[0:1, :] * x2


def _const_spec(shape):
    n = len(shape)
    return pl.BlockSpec(shape, lambda *_: (0,) * n, pipeline_mode=pl.Buffered(1))


def _params():
    return pltpu.CompilerParams(
        dimension_semantics=("arbitrary", "arbitrary"),
        vmem_limit_bytes=VMEM_LIMIT_BYTES)


def _mixer_kernel(h_ref, g_ref, win_ref, cw_ref, wout_ref, o_ref,
                  xn_sc, z_sc, ch_sc, halo_sc, m_sc):
    ts = h_ref.shape[0]

    @pl.when(pl.program_id(1) == 0)
    def _():
        halo_sc[...] = jnp.zeros_like(halo_sc)

    xn_sc[...] = _rms(h_ref[...], g_ref[...]).astype(BF16)
    for k in range(A_NCHUNK):
        z_sc[...] = _dot(xn_sc[...], win_ref[k])
        ch_sc[pl.ds(0, HALO), :] = halo_sc[k]
        ch_sc[pl.ds(HALO, ts), :] = z_sc[:, A_CHUNK:2 * A_CHUNK] * z_sc[:, 2 * A_CHUNK:]
        y = _causal_conv3(ch_sc, cw_ref[k], ts)
        halo_sc[k] = ch_sc[pl.ds(ts, HALO), :]
        m_sc[k] = (z_sc[:, :A_CHUNK] * y).astype(BF16)
    acc = _dot(m_sc[0], wout_ref[0])
    for k in range(1, A_NCHUNK):
        acc = acc + _dot(m_sc[k], wout_ref[k])
    o_ref[...] = h_ref[...] + acc


def _mixer(h, g, win_r, cw_r, wout_r):
    b, s, d = h.shape
    ts = SEQ_TILE
    tok = pl.BlockSpec((None, ts, d), lambda i, j: (i, j, 0))
    return pl.pallas_call(
        _mixer_kernel,
        grid=(b, s // ts),
        in_specs=[tok, _const_spec(g.shape), _const_spec(win_r.shape),
                  _const_spec(cw_r.shape), _const_spec(wout_r.shape)],
        out_specs=tok,
        out_shape=jax.ShapeDtypeStruct(h.shape, F32),
        scratch_shapes=[
            pltpu.VMEM((ts, d), BF16),
            pltpu.VMEM((ts, 3 * A_CHUNK), F32),
            pltpu.VMEM((ts + HALO, A_CHUNK), F32),
            pltpu.VMEM((A_NCHUNK, HALO, A_CHUNK), F32),
            pltpu.VMEM((A_NCHUNK, ts, A_CHUNK), BF16),
        ],
        compiler_params=_params(),
        name="short_conv_mixer",
    )(h, g, win_r, cw_r, wout_r)


def _ffn_kernel(*refs, has_oproj, has_final):
    refs = list(refs)
    h_ref = refs.pop(0)
    if has_oproj:
        o_ref, wo_ref = refs.pop(0), refs.pop(0)
    g_ref, wup_ref, cw_ref, wdn_ref = refs[:4]
    refs = refs[4:]
    if has_final:
        gf_ref = refs.pop(0)
    out_ref, xn_sc, up_sc, halo_sc, act_sc = refs
    ts = h_ref.shape[0]

    @pl.when(pl.program_id(1) == 0)
    def _():
        halo_sc[...] = jnp.zeros_like(halo_sc)

    h = h_ref[...]
    if has_oproj:
        h = h + _dot(o_ref[...], wo_ref[...])
    out_ref[...] = h
    xn_sc[...] = _rms(h, g_ref[...]).astype(BF16)

    def chunk(k, carry):
        up_sc[pl.ds(0, HALO), :] = halo_sc[k]
        up_sc[pl.ds(HALO, ts), :] = _dot(xn_sc[...], wup_ref[k])
        y = _causal_conv3(up_sc, cw_ref[k], ts)
        halo_sc[k] = up_sc[pl.ds(ts, HALO), :]
        gate = y[:, :F_CHUNK]
        act_sc[k] = (gate / (1.0 + jnp.exp(-gate)) * y[:, F_CHUNK:]).astype(BF16)
        return carry

    lax.fori_loop(0, F_NCHUNK, chunk, 0)

    acc = _dot(act_sc[0], wdn_ref[0])
    for k in range(1, F_NCHUNK):
        acc = acc + _dot(act_sc[k], wdn_ref[k])
    out = out_ref[...] + acc
    if has_final:
        out = _rms(out, gf_ref[...])
    out_ref[...] = out


def _ffn(h, g, wup_r, cw_r, wdn_r, *, o=None, wo=None, gf=None):
    b, s, d = h.shape
    ts = SEQ_TILE
    tok = pl.BlockSpec((None, ts, d), lambda i, j: (i, j, 0))
    args, specs = [h], [tok]
    if o is not None:
        args += [o, wo]
        specs += [tok, _const_spec(wo.shape)]
    args += [g, wup_r, cw_r, wdn_r]
    specs += [_const_spec(g.shape), _const_spec(wup_r.shape), _const_spec(cw_r.shape),
              _const_spec(wdn_r.shape)]
    if gf is not None:
        args.append(gf)
        specs.append(_const_spec(gf.shape))
    return pl.pallas_call(
        functools.partial(_ffn_kernel, has_oproj=o is not None, has_final=gf is not None),
        grid=(b, s // ts),
        in_specs=specs,
        out_specs=tok,
        out_shape=jax.ShapeDtypeStruct(h.shape, F32),
        scratch_shapes=[
            pltpu.VMEM((ts, d), BF16),
            pltpu.VMEM((ts + HALO, 2 * F_CHUNK), F32),
            pltpu.VMEM((F_NCHUNK, HALO, 2 * F_CHUNK), F32),
            pltpu.VMEM((F_NCHUNK, ts, F_CHUNK), BF16),
        ],
        compiler_params=_params(),
        name="conv_ffn",
    )(*args)


def _kv_kernel(h_ref, gin_ref, wdkv_ref, gkv_ref, wkt_ref, wv_ref, rope_ref, kt_ref, v_ref):
    ts = h_ref.shape[0]
    xn = _rms(h_ref[...], gin_ref[...]).astype(BF16)
    ckv = _dot(xn, wdkv_ref[...])
    cn = _rms(ckv[:, :KV_LORA_RANK], gkv_ref[...])
    v = _dot(cn.astype(BF16), wv_ref[...])
    kt = _dot(wkt_ref[...], cn.T.astype(BF16))
    r = ckv[:, KV_LORA_RANK:] * rope_ref[...]
    krt = (r + pltpu.roll(r, QK_ROPE_DIM, axis=1)).T.astype(BF16)
    ones = jnp.ones((ts, LANES), BF16)
    for hh in range(N_HEADS):
        kt_ref[hh, :QK_NOPE_DIM, :] = kt[hh * QK_NOPE_DIM:(hh + 1) * QK_NOPE_DIM, :].astype(BF16)
        kt_ref[hh, QK_NOPE_DIM:, :] = krt
        v_ref[hh, :, :V_HEAD_DIM] = v[:, hh * V_HEAD_DIM:(hh + 1) * V_HEAD_DIM].astype(BF16)
        v_ref[hh, :, V_HEAD_DIM:] = ones


def _shared_kv(h, gin, wdkv_x, gkv, wk_t, wv, rope_t):
    b, s, d = h.shape
    ts = KV_TILE
    return pl.pallas_call(
        _kv_kernel,
        grid=(b, s // ts),
        in_specs=[pl.BlockSpec((None, ts, d), lambda i, j: (i, j, 0)),
                  _const_spec(gin.shape), _const_spec(wdkv_x.shape), _const_spec(gkv.shape),
                  _const_spec(wk_t.shape), _const_spec(wv.shape),
                  pl.BlockSpec((ts, LANES), lambda i, j: (j, 0))],
        out_specs=[pl.BlockSpec((None, N_HEADS, None, HEAD_W, ts), lambda i, j: (i, 0, j, 0, 0)),
                   pl.BlockSpec((None, N_HEADS, ts, V_HEAD_DIM + LANES),
                                lambda i, j: (i, 0, j, 0))],
        out_shape=[jax.ShapeDtypeStruct((b, N_HEADS, s // ts, HEAD_W, ts), BF16),
                   jax.ShapeDtypeStruct((b, N_HEADS, s, V_HEAD_DIM + LANES), BF16)],
        compiler_params=_params(),
        name="shared_latent_kv",
    )(h, gin, wdkv_x, gkv, wk_t, wv, rope_t)


def _q_kernel(h_ref, g_ref, wdq_ref, gq_ref, wuq_ref, rope_ref, q_ref, *, qscale):
    xn = _rms(h_ref[...], g_ref[...]).astype(BF16)
    qn = _rms(_dot(xn, wdq_ref[...]), gq_ref[...]).astype(BF16)
    q = _dot(qn, wuq_ref[...])
    rope = rope_ref[...] * qscale
    for hh in range(N_HEADS):
        base = hh * HEAD_W
        q_ref[hh, :, :QK_NOPE_DIM] = (q[:, base:base + QK_NOPE_DIM] * qscale).astype(BF16)
        q_ref[hh, :, QK_NOPE_DIM:] = (q[:, base + QK_NOPE_DIM:base + HEAD_W] * rope).astype(BF16)


def _q_proj(h, g, wdq, gq, wuq_p, rope_t, qscale):
    b, s, d = h.shape
    ts = SEQ_TILE
    return pl.pallas_call(
        functools.partial(_q_kernel, qscale=qscale),
        grid=(b, s // ts),
        in_specs=[pl.BlockSpec((None, ts, d), lambda i, j: (i, j, 0)),
                  _const_spec(g.shape), _const_spec(wdq.shape), _const_spec(gq.shape),
                  _const_spec(wuq_p.shape),
                  pl.BlockSpec((ts, LANES), lambda i, j: (j, 0))],
        out_specs=pl.BlockSpec((None, N_HEADS, ts, HEAD_W), lambda i, j: (i, 0, j, 0)),
        out_shape=jax.ShapeDtypeStruct((b, N_HEADS, s, HEAD_W), BF16),
        compiler_params=_params(),
        name="mla_q_proj",
    )(h, g, wdq, gq, wuq_p, rope_t)


def _attn_kernel(q_ref, kt_ref, v_ref, o_ref, sa_sc, sb_sc, m_sc, acc_sc):
    qi = pl.program_id(2)
    tq = q_ref.shape[0]
    tk = KV_TILE
    q = q_ref[...]

    def scores(j):
        return _dot(q, kt_ref[j])

    def tile(t):
        return jnp.where(t == 0, qi, t - 1)

    def softmax_pv(s_ref, j):
        m_old = m_sc[...]
        cols = [s_ref[:, c * LANES:(c + 1) * LANES] for c in range(tk // LANES)]
        mx = functools.reduce(jnp.maximum, cols)
        m_new = jnp.maximum(m_old, jnp.max(mx, axis=-1, keepdims=True))
        alpha = jnp.exp2(m_old - m_new)
        p = jnp.concatenate([jnp.exp2(c - m_new) for c in cols], axis=1).astype(BF16)
        start = pl.multiple_of(j * tk, tk)
        pv = _dot(p, v_ref[pl.ds(start, tk), :])
        acc_sc[:, :V_HEAD_DIM] = alpha * acc_sc[:, :V_HEAD_DIM] + pv[:, :V_HEAD_DIM]
        acc_sc[:, V_HEAD_DIM:] = alpha * acc_sc[:, V_HEAD_DIM:] + pv[:, V_HEAD_DIM:]
        m_sc[...] = m_new

    row = lax.broadcasted_iota(jnp.int32, (tq, tk), 0)
    col = lax.broadcasted_iota(jnp.int32, (tq, tk), 1)
    sa_sc[...] = jnp.where(col <= row, scores(qi), MASK_VALUE)
    m_sc[...] = jnp.full(m_sc.shape, MASK_VALUE, F32)
    acc_sc[...] = jnp.zeros_like(acc_sc)

    def step(t, cur_sc, nxt_sc):
        nxt_sc[...] = scores(tile(t + 1))
        softmax_pv(cur_sc, tile(t))

    def pair(i, carry):
        step(2 * i, sa_sc, sb_sc)
        step(2 * i + 1, sb_sc, sa_sc)
        return carry

    lax.fori_loop(0, qi // 2, pair, 0)
    odd = lax.rem(qi, 2) == 1

    @pl.when(odd)
    def _():
        step(qi - 1, sa_sc, sb_sc)
        softmax_pv(sb_sc, tile(qi))

    @pl.when(jnp.logical_not(odd))
    def _():
        softmax_pv(sa_sc, tile(qi))

    o_ref[...] = (acc_sc[:, :V_HEAD_DIM] / acc_sc[:, V_HEAD_DIM:]).astype(o_ref.dtype)


def _attention(q, kt, v):
    b, nh, s, w = q.shape
    tq = Q_TILE
    vw = v.shape[-1]
    return pl.pallas_call(
        _attn_kernel,
        grid=(b, nh, s // tq),
        in_specs=[pl.BlockSpec((None, None, tq, w), lambda i, h, j: (i, h, j, 0)),
                  pl.BlockSpec((None, None) + kt.shape[2:], lambda i, h, j: (i, h, 0, 0, 0)),
                  pl.BlockSpec((None, None, s, vw), lambda i, h, j: (i, h, 0, 0))],
        out_specs=pl.BlockSpec((None, tq, V_HEAD_DIM), lambda i, h, j: (i, j, h)),
        out_shape=jax.ShapeDtypeStruct((b, s, nh * V_HEAD_DIM), BF16),
        scratch_shapes=[pltpu.VMEM((tq, KV_TILE), F32), pltpu.VMEM((tq, KV_TILE), F32),
                        pltpu.VMEM((tq, LANES), F32), pltpu.VMEM((tq, vw), F32)],
        compiler_params=pltpu.CompilerParams(
            dimension_semantics=("arbitrary", "arbitrary", "arbitrary"),
            vmem_limit_bytes=VMEM_LIMIT_BYTES),
        name="mla_flash_attention",
    )(q, kt, v)


def _chunk_cols(w, parts, chunk):
    k, n = w.shape
    c = n // parts
    w = w.reshape(k, parts, c // chunk, chunk)
    return jnp.transpose(w, (2, 0, 1, 3)).reshape(c // chunk, k, parts * chunk)


def _rope_table(seq):
    inv = 1.0 / (ROPE_THETA ** (jnp.arange(0, QK_ROPE_DIM, 2, dtype=F32) / QK_ROPE_DIM))
    ang = jnp.arange(seq, dtype=F32)[:, None] * inv[None, :]
    cos, sin = jnp.cos(ang), jnp.sin(ang)
    return jnp.concatenate([cos, cos, -sin, sin], axis=-1)


def _swap_halves(w):
    half = w.shape[-1] // 2
    return jnp.concatenate([w[..., half:], w[..., :half]], axis=-1)


def kernel(x, a_mix_norm, a_w_in, a_conv, a_w_out, b_mix_norm, b_w_dq, b_q_norm, b_w_uq, b_w_o,
           kv_in_norm, w_dkv, kv_norm, w_ukv, ffn_norm, ffn_w_up, ffn_conv, ffn_w_down, final_norm):
    seq = x.shape[1]
    rope_t = _rope_table(seq)
    qscale = (QK_NOPE_DIM + QK_ROPE_DIM) ** -0.5 * math.log2(math.e)

    def row(v):
        return v.reshape(1, -1)

    h = x
    kk = vv = None
    for layer in range(N_A_LAYERS + N_B_LAYERS):
        o = wo = None
        if layer < N_A_LAYERS:
            win_r = _chunk_cols(a_w_in[layer].astype(BF16), 3, A_CHUNK)
            cw_r = jnp.transpose(a_conv[layer].reshape(CONV_WIDTH, A_NCHUNK, A_CHUNK), (1, 0, 2))
            wout_r = a_w_out[layer].astype(BF16).reshape(A_NCHUNK, A_CHUNK, D_MODEL)
            h = _mixer(h, row(a_mix_norm[layer]), win_r, cw_r, wout_r)
        else:
            j = layer - N_A_LAYERS
            wuq = b_w_uq[j].reshape(Q_LORA_RANK, N_HEADS, QK_NOPE_DIM + QK_ROPE_DIM)
            wuq_rope = wuq[..., QK_NOPE_DIM:]
            wuq_p = jnp.concatenate([wuq[..., :QK_NOPE_DIM], wuq_rope, _swap_halves(wuq_rope)],
                                    axis=-1).reshape(Q_LORA_RANK, N_HEADS * HEAD_W).astype(BF16)
            q = _q_proj(h, row(b_mix_norm[j]), b_w_dq[j].astype(BF16), row(b_q_norm[j]), wuq_p,
                        rope_t, qscale)
            o = _attention(q, kk, vv)
            wo = b_w_o[j].astype(BF16)
        wup_r = _chunk_cols(ffn_w_up[layer].astype(BF16), 2, F_CHUNK)
        fcw_r = jnp.transpose(
            ffn_conv[layer].reshape(CONV_WIDTH, 2, F_NCHUNK, F_CHUNK), (2, 0, 1, 3)
        ).reshape(F_NCHUNK, CONV_WIDTH, 2 * F_CHUNK)
        wdn_r = ffn_w_down[layer].astype(BF16).reshape(F_NCHUNK, F_CHUNK, D_MODEL)
        last = layer == N_A_LAYERS + N_B_LAYERS - 1
        h = _ffn(h, row(ffn_norm[layer]), wup_r, fcw_r, wdn_r, o=o, wo=wo,
                 gf=row(final_norm) if last else None)
        if layer == N_A_LAYERS - 1:
            k_rope_w = w_dkv[:, KV_LORA_RANK:]
            wdkv_x = jnp.concatenate([w_dkv, _swap_halves(k_rope_w)], axis=-1).astype(BF16)
            wukv = w_ukv.reshape(KV_LORA_RANK, N_HEADS, QK_NOPE_DIM + V_HEAD_DIM)
            wk_t = wukv[..., :QK_NOPE_DIM].reshape(KV_LORA_RANK, -1).T.astype(BF16)
            wv = wukv[..., QK_NOPE_DIM:].reshape(KV_LORA_RANK, -1).astype(BF16)
            kk, vv = _shared_kv(h, row(kv_in_norm), wdkv_x, row(kv_norm), wk_t, wv, rope_t)
    return h
```

```python
import functools
import math

import jax
import jax.numpy as jnp
from jax import lax
from jax.experimental import pallas as pl
from jax.experimental.pallas import tpu as pltpu

D_MODEL = 1024
N_A_LAYERS = 2
N_B_LAYERS = 2
CONV_WIDTH = 3
D_FF = 2816
N_HEADS = 8
QK_NOPE_DIM = 128
QK_ROPE_DIM = 64
V_HEAD_DIM = 128
Q_LORA_RANK = 512
KV_LORA_RANK = 256
ROPE_THETA = 10000.0
RMS_EPS = 1e-6

LANES = 128
SUBLANES = 8
MXU_DIM = 256
VMEM_LIMIT_BYTES = 56 * 1024 * 1024

SEQ_TILE = 512
HALO = SUBLANES
A_CHUNK = MXU_DIM
A_NCHUNK = D_MODEL // A_CHUNK
F_CHUNK = MXU_DIM
F_NCHUNK = D_FF // F_CHUNK
HEAD_W = QK_NOPE_DIM + 2 * QK_ROPE_DIM
Q_TILE = 512
KV_TILE = 512
MASK_VALUE = -1e30

BF16 = jnp.bfloat16
F32 = jnp.float32


def _rms(x, g):
    ms = jnp.mean(x * x, axis=-1, keepdims=True)
    return x * lax.rsqrt(ms + RMS_EPS) * g


def _dot(a, b):
    return jnp.dot(a, b, preferred_element_type=F32)


def _causal_conv3(buf_ref, cw, rows):
    x0 = buf_ref[pl.ds(HALO, rows), :]
    x1 = buf_ref[pl.ds(HALO - 1, rows), :]
    x2 = buf_ref[pl.ds(HALO - 2, rows), :]
    return cw[2:3, :] * x0 + cw[1:2, :] * x1 + cw[0:1, :] * x2


def _const_spec(shape):
    n = len(shape)
    return pl.BlockSpec(shape, lambda *_: (0,) * n, pipeline_mode=pl.Buffered(1))


def _params():
    return pltpu.CompilerParams(
        dimension_semantics=("arbitrary", "arbitrary"),
        vmem_limit_bytes=VMEM_LIMIT_BYTES)


def _mixer_kernel(h_ref, g_ref, win_ref, cw_ref, wout_ref, o_ref,
                  xn_sc, z_sc, ch_sc, halo_sc, m_sc):
    ts = h_ref.shape[0]

    @pl.when(pl.program_id(1) == 0)
    def _():
        halo_sc[...] = jnp.zeros_like(halo_sc)

    xn_sc[...] = _rms(h_ref[...], g_ref[...]).astype(BF16)
    for k in range(A_NCHUNK):
        z_sc[...] = _dot(xn_sc[...], win_ref[k])
        ch_sc[pl.ds(0, HALO), :] = halo_sc[k]
        ch_sc[pl.ds(HALO, ts), :] = z_sc[:, A_CHUNK:2 * A_CHUNK] * z_sc[:, 2 * A_CHUNK:]
        y = _causal_conv3(ch_sc, cw_ref[k], ts)
        halo_sc[k] = ch_sc[pl.ds(ts, HALO), :]
        m_sc[k] = (z_sc[:, :A_CHUNK] * y).astype(BF16)
    acc = _dot(m_sc[0], wout_ref[0])
    for k in range(1, A_NCHUNK):
        acc = acc + _dot(m_sc[k], wout_ref[k])
    o_ref[...] = h_ref[...] + acc


def _mixer(h, g, win_r, cw_r, wout_r):
    b, s, d = h.shape
    ts = SEQ_TILE
    tok = pl.BlockSpec((None, ts, d), lambda i, j: (i, j, 0))
    return pl.pallas_call(
        _mixer_kernel,
        grid=(b, s // ts),
        in_specs=[tok, _const_spec(g.shape), _const_spec(win_r.shape),
                  _const_spec(cw_r.shape), _const_spec(wout_r.shape)],
        out_specs=tok,
        out_shape=jax.ShapeDtypeStruct(h.shape, F32),
        scratch_shapes=[
            pltpu.VMEM((ts, d), BF16),
            pltpu.VMEM((ts, 3 * A_CHUNK), F32),
            pltpu.VMEM((ts + HALO, A_CHUNK), F32),
            pltpu.VMEM((A_NCHUNK, HALO, A_CHUNK), F32),
            pltpu.VMEM((A_NCHUNK, ts, A_CHUNK), BF16),
        ],
        compiler_params=_params(),
        name="short_conv_mixer",
    )(h, g, win_r, cw_r, wout_r)


def _ffn_kernel(*refs, has_oproj, has_final):
    refs = list(refs)
    h_ref = refs.pop(0)
    if has_oproj:
        o_ref, wo_ref = refs.pop(0), refs.pop(0)
    g_ref, wup_ref, cw_ref, wdn_ref = refs[:4]
    refs = refs[4:]
    if has_final:
        gf_ref = refs.pop(0)
    out_ref, xn_sc, upa_sc, upb_sc, halo_sc, act_sc = refs
    ts = h_ref.shape[0]

    @pl.when(pl.program_id(1) == 0)
    def _():
        halo_sc[...] = jnp.zeros_like(halo_sc)

    h = h_ref[...]
    if has_oproj:
        h = h + _dot(o_ref[...], wo_ref[...])
    out_ref[...] = h
    xn_sc[...] = _rms(h, g_ref[...]).astype(BF16)

    def up_into(buf, k):
        buf[pl.ds(0, HALO), :] = halo_sc[k]
        buf[pl.ds(HALO, ts), :] = _dot(xn_sc[...], wup_ref[k])

    def gate_from(buf, k):
        y = _causal_conv3(buf, cw_ref[k], ts)
        halo_sc[k] = buf[pl.ds(ts, HALO), :]
        hg = y[:, :F_CHUNK]
        act_sc[k] = ((hg + hg * jnp.tanh(hg)) * y[:, F_CHUNK:]).astype(BF16)

    bufs = (upa_sc, upb_sc)
    up_into(bufs[0], 0)
    for k in range(F_NCHUNK):
        if k + 1 < F_NCHUNK:
            up_into(bufs[(k + 1) % 2], k + 1)
        gate_from(bufs[k % 2], k)

    acc = _dot(act_sc[0], wdn_ref[0])
    for k in range(1, F_NCHUNK):
        acc = acc + _dot(act_sc[k], wdn_ref[k])
    out = out_ref[...] + acc
    if has_final:
        out = _rms(out, gf_ref[...])
    out_ref[...] = out


def _ffn(h, g, wup_r, cw_r, wdn_r, *, o=None, wo=None, gf=None):
    b, s, d = h.shape
    ts = SEQ_TILE
    tok = pl.BlockSpec((None, ts, d), lambda i, j: (i, j, 0))
    args, specs = [h], [tok]
    if o is not None:
        args += [o, wo]
        specs += [tok, _const_spec(wo.shape)]
    args += [g, wup_r, cw_r, wdn_r]
    specs += [_const_spec(g.shape), _const_spec(wup_r.shape), _const_spec(cw_r.shape),
              _const_spec(wdn_r.shape)]
    if gf is not None:
        args.append(gf)
        specs.append(_const_spec(gf.shape))
    return pl.pallas_call(
        functools.partial(_ffn_kernel, has_oproj=o is not None, has_final=gf is not None),
        grid=(b, s // ts),
        in_specs=specs,
        out_specs=tok,
        out_shape=jax.ShapeDtypeStruct(h.shape, F32),
        scratch_shapes=[
            pltpu.VMEM((ts, d), BF16),
            pltpu.VMEM((ts + HALO, 2 * F_CHUNK), F32),
            pltpu.VMEM((ts + HALO, 2 * F_CHUNK), F32),
            pltpu.VMEM((F_NCHUNK, HALO, 2 * F_CHUNK), F32),
            pltpu.VMEM((F_NCHUNK, ts, F_CHUNK), BF16),
        ],
        compiler_params=_params(),
        name="conv_ffn",
    )(*args)


def _kv_kernel(h_ref, gin_ref, wdkv_ref, gkv_ref, wkt_ref, wv_ref, rope_ref, kt_ref, v_ref):
    ts = h_ref.shape[0]
    xn = _rms(h_ref[...], gin_ref[...]).astype(BF16)
    ckv = _dot(xn, wdkv_ref[...])
    cn = _rms(ckv[:, :KV_LORA_RANK], gkv_ref[...])
    v = _dot(cn.astype(BF16), wv_ref[...])
    kt = _dot(wkt_ref[...], cn.T.astype(BF16))
    r = ckv[:, KV_LORA_RANK:] * rope_ref[...]
    krt = (r + pltpu.roll(r, QK_ROPE_DIM, axis=1)).T.astype(BF16)
    ones = jnp.ones((ts, LANES), BF16)
    for hh in range(N_HEADS):
        kt_ref[hh, :QK_NOPE_DIM, :] = kt[hh * QK_NOPE_DIM:(hh + 1) * QK_NOPE_DIM, :].astype(BF16)
        kt_ref[hh, QK_NOPE_DIM:, :] = krt
        v_ref[hh, :, :V_HEAD_DIM] = v[:, hh * V_HEAD_DIM:(hh + 1) * V_HEAD_DIM].astype(BF16)
        v_ref[hh, :, V_HEAD_DIM:] = ones


def _shared_kv(h, gin, wdkv_x, gkv, wk_t, wv, rope_t):
    b, s, d = h.shape
    ts = KV_TILE
    return pl.pallas_call(
        _kv_kernel,
        grid=(b, s // ts),
        in_specs=[pl.BlockSpec((None, ts, d), lambda i, j: (i, j, 0)),
                  _const_spec(gin.shape), _const_spec(wdkv_x.shape), _const_spec(gkv.shape),
                  _const_spec(wk_t.shape), _const_spec(wv.shape),
                  pl.BlockSpec((ts, LANES), lambda i, j: (j, 0))],
        out_specs=[pl.BlockSpec((None, N_HEADS, None, HEAD_W, ts), lambda i, j: (i, 0, j, 0, 0)),
                   pl.BlockSpec((None, N_HEADS, ts, V_HEAD_DIM + LANES),
                                lambda i, j: (i, 0, j, 0))],
        out_shape=[jax.ShapeDtypeStruct((b, N_HEADS, s // ts, HEAD_W, ts), BF16),
                   jax.ShapeDtypeStruct((b, N_HEADS, s, V_HEAD_DIM + LANES), BF16)],
        compiler_params=_params(),
        name="shared_latent_kv",
    )(h, gin, wdkv_x, gkv, wk_t, wv, rope_t)


def _q_kernel(h_ref, g_ref, wdq_ref, gq_ref, wuq_ref, rope_ref, q_ref, *, qscale):
    xn = _rms(h_ref[...], g_ref[...]).astype(BF16)
    qn = _rms(_dot(xn, wdq_ref[...]), gq_ref[...]).astype(BF16)
    q = _dot(qn, wuq_ref[...])
    rope = rope_ref[...] * qscale
    for hh in range(N_HEADS):
        base = hh * HEAD_W
        q_ref[hh, :, :QK_NOPE_DIM] = (q[:, base:base + QK_NOPE_DIM] * qscale).astype(BF16)
        q_ref[hh, :, QK_NOPE_DIM:] = (q[:, base + QK_NOPE_DIM:base + HEAD_W] * rope).astype(BF16)


def _q_proj(h, g, wdq, gq, wuq_p, rope_t, qscale):
    b, s, d = h.shape
    ts = SEQ_TILE
    return pl.pallas_call(
        functools.partial(_q_kernel, qscale=qscale),
        grid=(b, s // ts),
        in_specs=[pl.BlockSpec((None, ts, d), lambda i, j: (i, j, 0)),
                  _const_spec(g.shape), _const_spec(wdq.shape), _const_spec(gq.shape),
                  _const_spec(wuq_p.shape),
                  pl.BlockSpec((ts, LANES), lambda i, j: (j, 0))],
        out_specs=pl.BlockSpec((None, N_HEADS, ts, HEAD_W), lambda i, j: (i, 0, j, 0)),
        out_shape=jax.ShapeDtypeStruct((b, N_HEADS, s, HEAD_W), BF16),
        compiler_params=_params(),
        name="mla_q_proj",
    )(h, g, wdq, gq, wuq_p, rope_t)


def _attn_kernel(q_ref, kt_ref, v_ref, o_ref, sa_sc, sb_sc, m_sc, acc_sc):
    qi = pl.program_id(2)
    tq = q_ref.shape[0]
    tk = KV_TILE
    q = q_ref[...]

    def scores(j):
        return _dot(q, kt_ref[j])

    def tile(t):
        return jnp.where(t == 0, qi, t - 1)

    def softmax_pv(s_ref, j):
        m_old = m_sc[...]
        cols = [s_ref[:, c * LANES:(c + 1) * LANES] for c in range(tk // LANES)]
        mx = functools.reduce(jnp.maximum, cols)
        m_new = jnp.maximum(m_old, jnp.max(mx, axis=-1, keepdims=True))
        alpha = jnp.exp2(m_old - m_new)
        p = jnp.concatenate([jnp.exp2(c - m_new) for c in cols], axis=1).astype(BF16)
        start = pl.multiple_of(j * tk, tk)
        pv = _dot(p, v_ref[pl.ds(start, tk), :])
        acc_sc[:, :V_HEAD_DIM] = alpha * acc_sc[:, :V_HEAD_DIM] + pv[:, :V_HEAD_DIM]
        acc_sc[:, V_HEAD_DIM:] = alpha * acc_sc[:, V_HEAD_DIM:] + pv[:, V_HEAD_DIM:]
        m_sc[...] = m_new

    row = lax.broadcasted_iota(jnp.int32, (tq, tk), 0)
    col = lax.broadcasted_iota(jnp.int32, (tq, tk), 1)
    sa_sc[...] = jnp.where(col <= row, scores(qi), MASK_VALUE)
    m_sc[...] = jnp.full(m_sc.shape, MASK_VALUE, F32)
    acc_sc[...] = jnp.zeros_like(acc_sc)

    def step(t, cur_sc, nxt_sc):
        nxt_sc[...] = scores(tile(t + 1))
        softmax_pv(cur_sc, tile(t))

    def pair(i, carry):
        step(2 * i, sa_sc, sb_sc)
        step(2 * i + 1, sb_sc, sa_sc)
        return carry

    lax.fori_loop(0, qi // 2, pair, 0)
    odd = lax.rem(qi, 2) == 1

    @pl.when(odd)
    def _():
        step(qi - 1, sa_sc, sb_sc)
        softmax_pv(sb_sc, tile(qi))

    @pl.when(jnp.logical_not(odd))
    def _():
        softmax_pv(sa_sc, tile(qi))

    o_ref[...] = (acc_sc[:, :V_HEAD_DIM] / acc_sc[:, V_HEAD_DIM:]).astype(o_ref.dtype)


def _attention(q, kt, v):
    b, nh, s, w = q.shape
    tq = Q_TILE
    vw = v.shape[-1]
    return pl.pallas_call(
        _attn_kernel,
        grid=(b, nh, s // tq),
        in_specs=[pl.BlockSpec((None, None, tq, w), lambda i, h, j: (i, h, j, 0)),
                  pl.BlockSpec((None, None) + kt.shape[2:], lambda i, h, j: (i, h, 0, 0, 0)),
                  pl.BlockSpec((None, None, s, vw), lambda i, h, j: (i, h, 0, 0))],
        out_specs=pl.BlockSpec((None, tq, V_HEAD_DIM), lambda i, h, j: (i, j, h)),
        out_shape=jax.ShapeDtypeStruct((b, s, nh * V_HEAD_DIM), BF16),
        scratch_shapes=[pltpu.VMEM((tq, KV_TILE), F32), pltpu.VMEM((tq, KV_TILE), F32),
                        pltpu.VMEM((tq, LANES), F32), pltpu.VMEM((tq, vw), F32)],
        compiler_params=pltpu.CompilerParams(
            dimension_semantics=("arbitrary", "arbitrary", "arbitrary"),
            vmem_limit_bytes=VMEM_LIMIT_BYTES),
        name="mla_flash_attention",
    )(q, kt, v)


def _chunk_cols(w, parts, chunk):
    k, n = w.shape
    c = n // parts
    w = w.reshape(k, parts, c // chunk, chunk)
    return jnp.transpose(w, (2, 0, 1, 3)).reshape(c // chunk, k, parts * chunk)


def _rope_table(seq):
    inv = 1.0 / (ROPE_THETA ** (jnp.arange(0, QK_ROPE_DIM, 2, dtype=F32) / QK_ROPE_DIM))
    ang = jnp.arange(seq, dtype=F32)[:, None] * inv[None, :]
    cos, sin = jnp.cos(ang), jnp.sin(ang)
    return jnp.concatenate([cos, cos, -sin, sin], axis=-1)


def _swap_halves(w):
    half = w.shape[-1] // 2
    return jnp.concatenate([w[..., half:], w[..., :half]], axis=-1)


def kernel(x, a_mix_norm, a_w_in, a_conv, a_w_out, b_mix_norm, b_w_dq, b_q_norm, b_w_uq, b_w_o,
           kv_in_norm, w_dkv, kv_norm, w_ukv, ffn_norm, ffn_w_up, ffn_conv, ffn_w_down, final_norm):
    seq = x.shape[1]
    rope_t = _rope_table(seq)
    qscale = (QK_NOPE_DIM + QK_ROPE_DIM) ** -0.5 * math.log2(math.e)

    def row(v):
        return v.reshape(1, -1)

    h = x
    kk = vv = None
    for layer in range(N_A_LAYERS + N_B_LAYERS):
        o = wo = None
        if layer < N_A_LAYERS:
            win_r = _chunk_cols(a_w_in[layer].astype(BF16), 3, A_CHUNK)
            cw_r = jnp.transpose(a_conv[layer].reshape(CONV_WIDTH, A_NCHUNK, A_CHUNK), (1, 0, 2))
            wout_r = a_w_out[layer].astype(BF16).reshape(A_NCHUNK, A_CHUNK, D_MODEL)
            h = _mixer(h, row(a_mix_norm[layer]), win_r, cw_r, wout_r)
        else:
            j = layer - N_A_LAYERS
            wuq = b_w_uq[j].reshape(Q_LORA_RANK, N_HEADS, QK_NOPE_DIM + QK_ROPE_DIM)
            wuq_rope = wuq[..., QK_NOPE_DIM:]
            wuq_p = jnp.concatenate([wuq[..., :QK_NOPE_DIM], wuq_rope, _swap_halves(wuq_rope)],
                                    axis=-1).reshape(Q_LORA_RANK, N_HEADS * HEAD_W).astype(BF16)
            q = _q_proj(h, row(b_mix_norm[j]), b_w_dq[j].astype(BF16), row(b_q_norm[j]), wuq_p,
                        rope_t, qscale)
            o = _attention(q, kk, vv)
            wo = b_w_o[j].astype(BF16)
        wup_r = _chunk_cols(ffn_w_up[layer].astype(BF16), 2, F_CHUNK)
        half_gate = jnp.array([0.5, 1.0], F32).reshape(1, 2, 1, 1)
        fcw_r = jnp.transpose(
            ffn_conv[layer].reshape(CONV_WIDTH, 2, F_NCHUNK, F_CHUNK) * half_gate, (2, 0, 1, 3)
        ).reshape(F_NCHUNK, CONV_WIDTH, 2 * F_CHUNK)
        wdn_r = ffn_w_down[layer].astype(BF16).reshape(F_NCHUNK, F_CHUNK, D_MODEL)
        last = layer == N_A_LAYERS + N_B_LAYERS - 1
        h = _ffn(h, row(ffn_norm[layer]), wup_r, fcw_r, wdn_r, o=o, wo=wo,
                 gf=row(final_norm) if last else None)
        if layer == N_A_LAYERS - 1:
            k_rope_w = w_dkv[:, KV_LORA_RANK:]
            wdkv_x = jnp.concatenate([w_dkv, _swap_halves(k_rope_w)], axis=-1).astype(BF16)
            wukv = w_ukv.reshape(KV_LORA_RANK, N_HEADS, QK_NOPE_DIM + V_HEAD_DIM)
            wk_t = wukv[..., :QK_NOPE_DIM].reshape(KV_LORA_RANK, -1).T.astype(BF16)
            wv = wukv[..., QK_NOPE_DIM:].reshape(KV_LORA_RANK, -1).astype(BF16)
            kk, vv = _shared_kv(h, row(kv_in_norm), wdkv_x, row(kv_norm), wk_t, wv, rope_t)
    return h
```

```python
import functools
import math

import jax
import jax.numpy as jnp
from jax import lax
from jax.experimental import pallas as pl
from jax.experimental.pallas import tpu as pltpu

D_MODEL = 1024
N_A_LAYERS = 2
N_B_LAYERS = 2
CONV_WIDTH = 3
D_FF = 2816
N_HEADS = 8
QK_NOPE_DIM = 128
QK_ROPE_DIM = 64
V_HEAD_DIM = 128
Q_LORA_RANK = 512
KV_LORA_RANK = 256
ROPE_THETA = 10000.0
RMS_EPS = 1e-6

LANES = 128
SUBLANES = 8
MXU_DIM = 256
VMEM_LIMIT_BYTES = 56 * 1024 * 1024

SEQ_TILE = 512
HALO = SUBLANES
A_CHUNK = MXU_DIM
A_NCHUNK = D_MODEL // A_CHUNK
F_CHUNK = MXU_DIM
F_NCHUNK = D_FF // F_CHUNK
HEAD_W = QK_NOPE_DIM + 2 * QK_ROPE_DIM
Q_TILE = 512
KV_TILE = 512
MASK_VALUE = -1e30

BF16 = jnp.bfloat16
F32 = jnp.float32


def _rms(x, g):
    ms = jnp.mean(x * x, axis=-1, keepdims=True)
    return x * lax.rsqrt(ms + RMS_EPS) * g


def _dot(a, b):
    return jnp.dot(a, b, preferred_element_type=F32)


def _causal_conv3(buf_ref, cw, rows):
    x0 = buf_ref[pl.ds(HALO, rows), :]
    x1 = buf_ref[pl.ds(HALO - 1, rows), :]
    x2 = buf_ref[pl.ds(HALO - 2, rows), :]
    return cw[2:3, :] * x0 + cw[1:2, :] * x1 + cw[0:1, :] * x2


def _const_spec(shape):
    n = len(shape)
    return pl.BlockSpec(shape, lambda *_: (0,) * n, pipeline_mode=pl.Buffered(1))


def _params():
    return pltpu.CompilerParams(
        dimension_semantics=("arbitrary", "arbitrary"),
        vmem_limit_bytes=VMEM_LIMIT_BYTES)


def _mixer_kernel(h_ref, g_ref, win_ref, cw_ref, wout_ref, o_ref,
                  xn_sc, z_sc, ch_sc, halo_sc, m_sc):
    ts = h_ref.shape[0]

    @pl.when(pl.program_id(1) == 0)
    def _():
        halo_sc[...] = jnp.zeros_like(halo_sc)

    xn_sc[...] = _rms(h_ref[...], g_ref[...]).astype(BF16)
    for k in range(A_NCHUNK):
        z_sc[...] = _dot(xn_sc[...], win_ref[k])
        ch_sc[pl.ds(0, HALO), :] = halo_sc[k]
        ch_sc[pl.ds(HALO, ts), :] = z_sc[:, A_CHUNK:2 * A_CHUNK] * z_sc[:, 2 * A_CHUNK:]
        y = _causal_conv3(ch_sc, cw_ref[k], ts)
        halo_sc[k] = ch_sc[pl.ds(ts, HALO), :]
        m_sc[k] = (z_sc[:, :A_CHUNK] * y).astype(BF16)
    acc = _dot(m_sc[0], wout_ref[0])
    for k in range(1, A_NCHUNK):
        acc = acc + _dot(m_sc[k], wout_ref[k])
    o_ref[...] = h_ref[...] + acc


def _mixer(h, g, win_r, cw_r, wout_r):
    b, s, d = h.shape
    ts = SEQ_TILE
    tok = pl.BlockSpec((None, ts, d), lambda i, j: (i, j, 0))
    return pl.pallas_call(
        _mixer_kernel,
        grid=(b, s // ts),
        in_specs=[tok, _const_spec(g.shape), _const_spec(win_r.shape),
                  _const_spec(cw_r.shape), _const_spec(wout_r.shape)],
        out_specs=tok,
        out_shape=jax.ShapeDtypeStruct(h.shape, F32),
        scratch_shapes=[
            pltpu.VMEM((ts, d), BF16),
            pltpu.VMEM((ts, 3 * A_CHUNK), F32),
            pltpu.VMEM((ts + HALO, A_CHUNK), F32),
            pltpu.VMEM((A_NCHUNK, HALO, A_CHUNK), F32),
            pltpu.VMEM((A_NCHUNK, ts, A_CHUNK), BF16),
        ],
        compiler_params=_params(),
        name="short_conv_mixer",
    )(h, g, win_r, cw_r, wout_r)


def _ffn_kernel(*refs, has_oproj, has_final):
    refs = list(refs)
    h_ref = refs.pop(0)
    if has_oproj:
        o_ref, wo_ref = refs.pop(0), refs.pop(0)
    g_ref, wup_ref, cw_ref, wdn_ref = refs[:4]
    refs = refs[4:]
    if has_final:
        gf_ref = refs.pop(0)
    out_ref, xn_sc, upa_sc, upb_sc, halo_sc, act_sc = refs
    ts = h_ref.shape[0]

    @pl.when(pl.program_id(1) == 0)
    def _():
        halo_sc[...] = jnp.zeros_like(halo_sc)

    h = h_ref[...]
    if has_oproj:
        h = h + _dot(o_ref[...], wo_ref[...])
    out_ref[...] = h
    xn_sc[...] = _rms(h, g_ref[...]).astype(BF16)

    def up_into(buf, k):
        buf[pl.ds(0, HALO), :] = halo_sc[k]
        buf[pl.ds(HALO, ts), :] = _dot(xn_sc[...], wup_ref[k])

    def gate_from(buf, k):
        y = _causal_conv3(buf, cw_ref[k], ts)
        halo_sc[k] = buf[pl.ds(ts, HALO), :]
        hg = y[:, :F_CHUNK]
        act_sc[k] = ((hg + hg * jnp.tanh(hg)) * y[:, F_CHUNK:]).astype(BF16)

    bufs = (upa_sc, upb_sc)
    up_into(bufs[0], 0)
    for k in range(F_NCHUNK):
        if k + 1 < F_NCHUNK:
            up_into(bufs[(k + 1) % 2], k + 1)
        gate_from(bufs[k % 2], k)

    acc = _dot(act_sc[0], wdn_ref[0])
    for k in range(1, F_NCHUNK):
        acc = acc + _dot(act_sc[k], wdn_ref[k])
    out = out_ref[...] + acc
    if has_final:
        out = _rms(out, gf_ref[...])
    out_ref[...] = out


def _ffn(h, g, wup_r, cw_r, wdn_r, *, o=None, wo=None, gf=None):
    b, s, d = h.shape
    ts = SEQ_TILE
    tok = pl.BlockSpec((None, ts, d), lambda i, j: (i, j, 0))
    args, specs = [h], [tok]
    if o is not None:
        args += [o, wo]
        specs += [tok, _const_spec(wo.shape)]
    args += [g, wup_r, cw_r, wdn_r]
    specs += [_const_spec(g.shape), _const_spec(wup_r.shape), _const_spec(cw_r.shape),
              _const_spec(wdn_r.shape)]
    if gf is not None:
        args.append(gf)
        specs.append(_const_spec(gf.shape))
    return pl.pallas_call(
        functools.partial(_ffn_kernel, has_oproj=o is not None, has_final=gf is not None),
        grid=(b, s // ts),
        in_specs=specs,
        out_specs=tok,
        out_shape=jax.ShapeDtypeStruct(h.shape, F32),
        scratch_shapes=[
            pltpu.VMEM((ts, d), BF16),
            pltpu.VMEM((ts + HALO, 2 * F_CHUNK), F32),
            pltpu.VMEM((ts + HALO, 2 * F_CHUNK), F32),
            pltpu.VMEM((F_NCHUNK, HALO, 2 * F_CHUNK), F32),
            pltpu.VMEM((F_NCHUNK, ts, F_CHUNK), BF16),
        ],
        compiler_params=_params(),
        name="conv_ffn",
    )(*args)


def _kv_kernel(h_ref, gin_ref, wdkv_ref, gkv_ref, wkt_ref, wv_ref, rope_ref, kt_ref, v_ref):
    ts = h_ref.shape[0]
    xn = _rms(h_ref[...], gin_ref[...]).astype(BF16)
    ckv = _dot(xn, wdkv_ref[...])
    cn = _rms(ckv[:, :KV_LORA_RANK], gkv_ref[...])
    v = _dot(cn.astype(BF16), wv_ref[...])
    kt = _dot(wkt_ref[...], cn.T.astype(BF16))
    r = ckv[:, KV_LORA_RANK:] * rope_ref[...]
    krt = (r + pltpu.roll(r, QK_ROPE_DIM, axis=1)).T.astype(BF16)
    ones = jnp.ones((ts, LANES), BF16)
    for hh in range(N_HEADS):
        kt_ref[hh, :QK_NOPE_DIM, :] = kt[hh * QK_NOPE_DIM:(hh + 1) * QK_NOPE_DIM, :].astype(BF16)
        kt_ref[hh, QK_NOPE_DIM:, :] = krt
        v_ref[hh, :, :V_HEAD_DIM] = v[:, hh * V_HEAD_DIM:(hh + 1) * V_HEAD_DIM].astype(BF16)
        v_ref[hh, :, V_HEAD_DIM:] = ones


def _shared_kv(h, gin, wdkv_x, gkv, wk_t, wv, rope_t):
    b, s, d = h.shape
    ts = KV_TILE
    return pl.pallas_call(
        _kv_kernel,
        grid=(b, s // ts),
        in_specs=[pl.BlockSpec((None, ts, d), lambda i, j: (i, j, 0)),
                  _const_spec(gin.shape), _const_spec(wdkv_x.shape), _const_spec(gkv.shape),
                  _const_spec(wk_t.shape), _const_spec(wv.shape),
                  pl.BlockSpec((ts, LANES), lambda i, j: (j, 0))],
        out_specs=[pl.BlockSpec((None, N_HEADS, None, HEAD_W, ts), lambda i, j: (i, 0, j, 0, 0)),
                   pl.BlockSpec((None, N_HEADS, ts, V_HEAD_DIM + LANES),
                                lambda i, j: (i, 0, j, 0))],
        out_shape=[jax.ShapeDtypeStruct((b, N_HEADS, s // ts, HEAD_W, ts), BF16),
                   jax.ShapeDtypeStruct((b, N_HEADS, s, V_HEAD_DIM + LANES), BF16)],
        compiler_params=_params(),
        name="shared_latent_kv",
    )(h, gin, wdkv_x, gkv, wk_t, wv, rope_t)


def _q_kernel(h_ref, g_ref, wdq_ref, gq_ref, wuq_ref, rope_ref, q_ref, *, qscale):
    xn = _rms(h_ref[...], g_ref[...]).astype(BF16)
    qn = _rms(_dot(xn, wdq_ref[...]), gq_ref[...]).astype(BF16)
    q = _dot(qn, wuq_ref[...])
    rope = rope_ref[...] * qscale
    for hh in range(N_HEADS):
        base = hh * HEAD_W
        q_ref[hh, :, :QK_NOPE_DIM] = (q[:, base:base + QK_NOPE_DIM] * qscale).astype(BF16)
        q_ref[hh, :, QK_NOPE_DIM:] = (q[:, base + QK_NOPE_DIM:base + HEAD_W] * rope).astype(BF16)


def _q_proj(h, g, wdq, gq, wuq_p, rope_t, qscale):
    b, s, d = h.shape
    ts = SEQ_TILE
    return pl.pallas_call(
        functools.partial(_q_kernel, qscale=qscale),
        grid=(b, s // ts),
        in_specs=[pl.BlockSpec((None, ts, d), lambda i, j: (i, j, 0)),
                  _const_spec(g.shape), _const_spec(wdq.shape), _const_spec(gq.shape),
                  _const_spec(wuq_p.shape),
                  pl.BlockSpec((ts, LANES), lambda i, j: (j, 0))],
        out_specs=pl.BlockSpec((None, N_HEADS, ts, HEAD_W), lambda i, j: (i, 0, j, 0)),
        out_shape=jax.ShapeDtypeStruct((b, N_HEADS, s, HEAD_W), BF16),
        compiler_params=_params(),
        name="mla_q_proj",
    )(h, g, wdq, gq, wuq_p, rope_t)


def _attn_kernel(q_ref, kt_ref, v_ref, o_ref, sa_sc, sb_sc, m_sc, acc_sc):
    tq = Q_TILE
    tk = KV_TILE
    nq = q_ref.shape[0] // tq
    row = lax.broadcasted_iota(jnp.int32, (tq, tk), 0)
    col = lax.broadcasted_iota(jnp.int32, (tq, tk), 1)
    causal = col <= row

    def scores(pair, s_ref):
        i, j = pair
        s = _dot(q_ref[pl.ds(i * tq, tq), :], kt_ref[j])
        s_ref[...] = jnp.where(causal, s, MASK_VALUE) if i == j else s

    def softmax_pv(pair, s_ref):
        i, j = pair
        cols = [s_ref[:, c * LANES:(c + 1) * LANES] for c in range(tk // LANES)]
        m_cur = jnp.max(functools.reduce(jnp.maximum, cols), axis=-1, keepdims=True)
        if i == j:
            m_new = jnp.broadcast_to(m_cur, (tq, LANES))
        else:
            m_old = m_sc[i]
            m_new = jnp.maximum(m_old, m_cur)
            alpha = jnp.exp2(m_old - m_new)
        p = jnp.concatenate([jnp.exp2(c - m_new) for c in cols], axis=1).astype(BF16)
        pv = _dot(p, v_ref[pl.ds(j * tk, tk), :])
        if i == j:
            acc_sc[i] = pv
        else:
            acc_sc[i, :, :V_HEAD_DIM] = alpha * acc_sc[i, :, :V_HEAD_DIM] + pv[:, :V_HEAD_DIM]
            acc_sc[i, :, V_HEAD_DIM:] = alpha * acc_sc[i, :, V_HEAD_DIM:] + pv[:, V_HEAD_DIM:]
        m_sc[i] = m_new

    pairs = [(i, i) for i in range(nq)] + [(i, j) for i in range(1, nq) for j in range(i)]
    bufs = (sa_sc, sb_sc)
    scores(pairs[0], bufs[0])
    for t, pair in enumerate(pairs):
        if t + 1 < len(pairs):
            scores(pairs[t + 1], bufs[(t + 1) % 2])
        softmax_pv(pair, bufs[t % 2])
    for i in range(nq):
        o_ref[pl.ds(i * tq, tq), :] = (
            acc_sc[i, :, :V_HEAD_DIM] / acc_sc[i, :, V_HEAD_DIM:]).astype(o_ref.dtype)


def _attention(q, kt, v):
    b, nh, s, w = q.shape
    tq = Q_TILE
    nq = s // tq
    vw = v.shape[-1]
    return pl.pallas_call(
        _attn_kernel,
        grid=(b, nh),
        in_specs=[pl.BlockSpec((None, None, s, w), lambda i, h: (i, h, 0, 0)),
                  pl.BlockSpec((None, None) + kt.shape[2:], lambda i, h: (i, h, 0, 0, 0)),
                  pl.BlockSpec((None, None, s, vw), lambda i, h: (i, h, 0, 0))],
        out_specs=pl.BlockSpec((None, s, V_HEAD_DIM), lambda i, h: (i, 0, h)),
        out_shape=jax.ShapeDtypeStruct((b, s, nh * V_HEAD_DIM), BF16),
        scratch_shapes=[pltpu.VMEM((tq, KV_TILE), F32), pltpu.VMEM((tq, KV_TILE), F32),
                        pltpu.VMEM((nq, tq, LANES), F32), pltpu.VMEM((nq, tq, vw), F32)],
        compiler_params=pltpu.CompilerParams(
            dimension_semantics=("arbitrary", "arbitrary"),
            vmem_limit_bytes=VMEM_LIMIT_BYTES),
        name="mla_flash_attention",
    )(q, kt, v)


def _chunk_cols(w, parts, chunk):
    k, n = w.shape
    c = n // parts
    w = w.reshape(k, parts, c // chunk, chunk)
    return jnp.transpose(w, (2, 0, 1, 3)).reshape(c // chunk, k, parts * chunk)


def _rope_table(seq):
    inv = 1.0 / (ROPE_THETA ** (jnp.arange(0, QK_ROPE_DIM, 2, dtype=F32) / QK_ROPE_DIM))
    ang = jnp.arange(seq, dtype=F32)[:, None] * inv[None, :]
    cos, sin = jnp.cos(ang), jnp.sin(ang)
    return jnp.concatenate([cos, cos, -sin, sin], axis=-1)


def _swap_halves(w):
    half = w.shape[-1] // 2
    return jnp.concatenate([w[..., half:], w[..., :half]], axis=-1)


def kernel(x, a_mix_norm, a_w_in, a_conv, a_w_out, b_mix_norm, b_w_dq, b_q_norm, b_w_uq, b_w_o,
           kv_in_norm, w_dkv, kv_norm, w_ukv, ffn_norm, ffn_w_up, ffn_conv, ffn_w_down, final_norm):
    seq = x.shape[1]
    rope_t = _rope_table(seq)
    qscale = (QK_NOPE_DIM + QK_ROPE_DIM) ** -0.5 * math.log2(math.e)

    def row(v):
        return v.reshape(1, -1)

    h = x
    kk = vv = None
    for layer in range(N_A_LAYERS + N_B_LAYERS):
        o = wo = None
        if layer < N_A_LAYERS:
            win_r = _chunk_cols(a_w_in[layer].astype(BF16), 3, A_CHUNK)
            cw_r = jnp.transpose(a_conv[layer].reshape(CONV_WIDTH, A_NCHUNK, A_CHUNK), (1, 0, 2))
            wout_r = a_w_out[layer].astype(BF16).reshape(A_NCHUNK, A_CHUNK, D_MODEL)
            h = _mixer(h, row(a_mix_norm[layer]), win_r, cw_r, wout_r)
        else:
            j = layer - N_A_LAYERS
            wuq = b_w_uq[j].reshape(Q_LORA_RANK, N_HEADS, QK_NOPE_DIM + QK_ROPE_DIM)
            wuq_rope = wuq[..., QK_NOPE_DIM:]
            wuq_p = jnp.concatenate([wuq[..., :QK_NOPE_DIM], wuq_rope, _swap_halves(wuq_rope)],
                                    axis=-1).reshape(Q_LORA_RANK, N_HEADS * HEAD_W).astype(BF16)
            q = _q_proj(h, row(b_mix_norm[j]), b_w_dq[j].astype(BF16), row(b_q_norm[j]), wuq_p,
                        rope_t, qscale)
            o = _attention(q, kk, vv)
            wo = b_w_o[j].astype(BF16)
        wup_r = _chunk_cols(ffn_w_up[layer].astype(BF16), 2, F_CHUNK)
        half_gate = jnp.array([0.5, 1.0], F32).reshape(1, 2, 1, 1)
        fcw_r = jnp.transpose(
            ffn_conv[layer].reshape(CONV_WIDTH, 2, F_NCHUNK, F_CHUNK) * half_gate, (2, 0, 1, 3)
        ).reshape(F_NCHUNK, CONV_WIDTH, 2 * F_CHUNK)
        wdn_r = ffn_w_down[layer].astype(BF16).reshape(F_NCHUNK, F_CHUNK, D_MODEL)
        last = layer == N_A_LAYERS + N_B_LAYERS - 1
        h = _ffn(h, row(ffn_norm[layer]), wup_r, fcw_r, wdn_r, o=o, wo=wo,
                 gf=row(final_norm) if last else None)
        if layer == N_A_LAYERS - 1:
            k_rope_w = w_dkv[:, KV_LORA_RANK:]
            wdkv_x = jnp.concatenate([w_dkv, _swap_halves(k_rope_w)], axis=-1).astype(BF16)
            wukv = w_ukv.reshape(KV_LORA_RANK, N_HEADS, QK_NOPE_DIM + V_HEAD_DIM)
            wk_t = wukv[..., :QK_NOPE_DIM].reshape(KV_LORA_RANK, -1).T.astype(BF16)
            wv = wukv[..., QK_NOPE_DIM:].reshape(KV_LORA_RANK, -1).astype(BF16)
            kk, vv = _shared_kv(h, row(kv_in_norm), wdkv_x, row(kv_norm), wk_t, wv, rope_t)
    return h
```

```python
import functools
import math

import jax
import jax.numpy as jnp
from jax import lax
from jax.experimental import pallas as pl
from jax.experimental.pallas import tpu as pltpu

D_MODEL = 1024
N_A_LAYERS = 2
N_B_LAYERS = 2
CONV_WIDTH = 3
D_FF = 2816
N_HEADS = 8
QK_NOPE_DIM = 128
QK_ROPE_DIM = 64
V_HEAD_DIM = 128
Q_LORA_RANK = 512
KV_LORA_RANK = 256
ROPE_THETA = 10000.0
RMS_EPS = 1e-6

LANES = 128
SUBLANES = 8
MXU_DIM = 256
VMEM_LIMIT_BYTES = 56 * 1024 * 1024

SEQ_TILE = 512
HALO = SUBLANES
A_CHUNK = MXU_DIM
A_NCHUNK = D_MODEL // A_CHUNK
F_CHUNK = MXU_DIM
F_NCHUNK = D_FF // F_CHUNK
HEAD_W = QK_NOPE_DIM + 2 * QK_ROPE_DIM
Q_TILE = 512
KV_TILE = 512
MASK_VALUE = -1e30

BF16 = jnp.bfloat16
F32 = jnp.float32


def _rms(x, g):
    ms = jnp.mean(x * x, axis=-1, keepdims=True)
    return x * lax.rsqrt(ms + RMS_EPS) * g


def _dot(a, b):
    return jnp.dot(a, b, preferred_element_type=F32)


def _causal_conv3(buf_ref, cw, rows):
    x0 = buf_ref[pl.ds(HALO, rows), :]
    x1 = buf_ref[pl.ds(HALO - 1, rows), :]
    x2 = buf_ref[pl.ds(HALO - 2, rows), :]
    return cw[2:3, :] * x0 + cw[1:2, :] * x1 + cw[0:1, :] * x2


def _const_spec(shape):
    n = len(shape)
    return pl.BlockSpec(shape, lambda *_: (0,) * n, pipeline_mode=pl.Buffered(1))


def _params():
    return pltpu.CompilerParams(
        dimension_semantics=("arbitrary", "arbitrary"),
        vmem_limit_bytes=VMEM_LIMIT_BYTES)


def _mixer_kernel(h_ref, g_ref, win_ref, cw_ref, wout_ref, o_ref,
                  xn_sc, ch_sc, halo_sc, m_sc):
    ts, d = h_ref.shape

    @pl.when(pl.program_id(1) == 0)
    def _():
        halo_sc[...] = jnp.zeros_like(halo_sc)

    xn_sc[...] = _rms(h_ref[...], g_ref[...]).astype(BF16)

    def proj(part, k):
        lo = part * d + k * A_CHUNK
        return _dot(xn_sc[...], win_ref[:, lo:lo + A_CHUNK])

    for k in range(A_NCHUNK):
        ch_sc[pl.ds(0, HALO), :] = halo_sc[k]
        ch_sc[pl.ds(HALO, ts), :] = proj(1, k) * proj(2, k)
        y = _causal_conv3(ch_sc, cw_ref[:, k * A_CHUNK:(k + 1) * A_CHUNK], ts)
        halo_sc[k] = ch_sc[pl.ds(ts, HALO), :]
        m_sc[k] = (proj(0, k) * y).astype(BF16)
    acc = _dot(m_sc[0], wout_ref[0:A_CHUNK, :])
    for k in range(1, A_NCHUNK):
        acc = acc + _dot(m_sc[k], wout_ref[k * A_CHUNK:(k + 1) * A_CHUNK, :])
    o_ref[...] = h_ref[...] + acc


def _mixer(h, g, win, cw, wout):
    b, s, d = h.shape
    ts = SEQ_TILE
    tok = pl.BlockSpec((None, ts, d), lambda i, j: (i, j, 0))
    return pl.pallas_call(
        _mixer_kernel,
        grid=(b, s // ts),
        in_specs=[tok, _const_spec(g.shape), _const_spec(win.shape),
                  _const_spec(cw.shape), _const_spec(wout.shape)],
        out_specs=tok,
        out_shape=jax.ShapeDtypeStruct(h.shape, F32),
        scratch_shapes=[
            pltpu.VMEM((ts, d), BF16),
            pltpu.VMEM((ts + HALO, A_CHUNK), F32),
            pltpu.VMEM((A_NCHUNK, HALO, A_CHUNK), F32),
            pltpu.VMEM((A_NCHUNK, ts, A_CHUNK), BF16),
        ],
        compiler_params=_params(),
        name="short_conv_mixer",
    )(h, g, win, cw, wout)


def _ffn_kernel(*refs, has_oproj, has_final):
    refs = list(refs)
    h_ref = refs.pop(0)
    if has_oproj:
        o_ref, wo_ref = refs.pop(0), refs.pop(0)
    g_ref, wup_ref, cw_ref, wdn_ref = refs[:4]
    refs = refs[4:]
    if has_final:
        gf_ref = refs.pop(0)
    out_ref, xn_sc, upa_sc, upb_sc, halo_sc, actf_sc, act_sc = refs
    ts = h_ref.shape[0]

    @pl.when(pl.program_id(1) == 0)
    def _():
        halo_sc[...] = jnp.zeros_like(halo_sc)

    h = h_ref[...]
    if has_oproj:
        h = h + _dot(o_ref[...], wo_ref[...])
    out_ref[...] = h
    xn_sc[...] = _rms(h, g_ref[...]).astype(BF16)

    nslab = 2 * F_CHUNK // LANES
    half = ts // 2

    def up_into(buf, k):
        for part in range(2):
            lo = part * D_FF + k * F_CHUNK
            res = _dot(xn_sc[...], wup_ref[:, lo:lo + F_CHUNK])
            for c in range(nslab // 2):
                slab = part * (nslab // 2) + c
                buf[slab, pl.ds(0, HALO), :] = halo_sc[k, slab]
                buf[slab, pl.ds(HALO, ts), :] = res[:, c * LANES:(c + 1) * LANES]

    def gate_from(buf, k):
        def conv(slab, parity):
            part, c = divmod(slab, nslab // 2)
            lo = part * D_FF + k * F_CHUNK + c * LANES
            w = cw_ref[:, lo:lo + LANES]
            x = [buf[slab, pl.ds(HALO + parity - d, half, stride=2), :] for d in range(3)]
            return w[2:3, :] * x[0] + w[1:2, :] * x[1] + w[0:1, :] * x[2]

        for c in range(nslab // 2):
            for parity in range(2):
                hg = conv(c, parity)
                u = conv(c + nslab // 2, parity)
                actf_sc[c, pl.ds(parity, half, stride=2), :] = (hg + hg * jnp.tanh(hg)) * u
        for c in range(nslab):
            halo_sc[k, c] = buf[c, pl.ds(ts, HALO), :]
        act_sc[k] = jnp.concatenate([actf_sc[c] for c in range(nslab // 2)], axis=1).astype(BF16)

    bufs = (upa_sc, upb_sc)
    up_into(bufs[0], 0)
    for k in range(F_NCHUNK):
        if k + 1 < F_NCHUNK:
            up_into(bufs[(k + 1) % 2], k + 1)
        gate_from(bufs[k % 2], k)

    acc = _dot(act_sc[0], wdn_ref[0:F_CHUNK, :])
    for k in range(1, F_NCHUNK):
        acc = acc + _dot(act_sc[k], wdn_ref[k * F_CHUNK:(k + 1) * F_CHUNK, :])
    out = out_ref[...] + acc
    if has_final:
        out = _rms(out, gf_ref[...])
    out_ref[...] = out


def _ffn(h, g, wup, cw, wdn, *, o=None, wo=None, gf=None):
    b, s, d = h.shape
    ts = SEQ_TILE
    tok = pl.BlockSpec((None, ts, d), lambda i, j: (i, j, 0))
    args, specs = [h], [tok]
    if o is not None:
        args += [o, wo]
        specs += [tok, _const_spec(wo.shape)]
    args += [g, wup, cw, wdn]
    specs += [_const_spec(g.shape), _const_spec(wup.shape), _const_spec(cw.shape),
              _const_spec(wdn.shape)]
    if gf is not None:
        args.append(gf)
        specs.append(_const_spec(gf.shape))
    nslab = 2 * F_CHUNK // LANES
    return pl.pallas_call(
        functools.partial(_ffn_kernel, has_oproj=o is not None, has_final=gf is not None),
        grid=(b, s // ts),
        in_specs=specs,
        out_specs=tok,
        out_shape=jax.ShapeDtypeStruct(h.shape, F32),
        scratch_shapes=[
            pltpu.VMEM((ts, d), BF16),
            pltpu.VMEM((nslab, ts + HALO, LANES), F32),
            pltpu.VMEM((nslab, ts + HALO, LANES), F32),
            pltpu.VMEM((F_NCHUNK, nslab, HALO, LANES), F32),
            pltpu.VMEM((nslab // 2, ts, LANES), F32),
            pltpu.VMEM((F_NCHUNK, ts, F_CHUNK), BF16),
        ],
        compiler_params=_params(),
        name="conv_ffn",
    )(*args)


def _kv_kernel(h_ref, gin_ref, wdkv_ref, gkv_ref, wkt_ref, wv_ref, rope_ref, kt_ref, v_ref):
    ts = h_ref.shape[0]
    xn = _rms(h_ref[...], gin_ref[...]).astype(BF16)
    ckv = _dot(xn, wdkv_ref[...])
    cn = _rms(ckv[:, :KV_LORA_RANK], gkv_ref[...])
    v = _dot(cn.astype(BF16), wv_ref[...])
    kt = _dot(wkt_ref[...], cn.T.astype(BF16))
    r = ckv[:, KV_LORA_RANK:] * rope_ref[...]
    krt = (r + pltpu.roll(r, QK_ROPE_DIM, axis=1)).T.astype(BF16)
    ones = jnp.ones((ts, LANES), BF16)
    for hh in range(N_HEADS):
        kt_ref[hh, :QK_NOPE_DIM, :] = kt[hh * QK_NOPE_DIM:(hh + 1) * QK_NOPE_DIM, :].astype(BF16)
        kt_ref[hh, QK_NOPE_DIM:, :] = krt
        v_ref[hh, :, :V_HEAD_DIM] = v[:, hh * V_HEAD_DIM:(hh + 1) * V_HEAD_DIM].astype(BF16)
        v_ref[hh, :, V_HEAD_DIM:] = ones


def _shared_kv(h, gin, wdkv_x, gkv, wk_t, wv, rope_t):
    b, s, d = h.shape
    ts = KV_TILE
    return pl.pallas_call(
        _kv_kernel,
        grid=(b, s // ts),
        in_specs=[pl.BlockSpec((None, ts, d), lambda i, j: (i, j, 0)),
                  _const_spec(gin.shape), _const_spec(wdkv_x.shape), _const_spec(gkv.shape),
                  _const_spec(wk_t.shape), _const_spec(wv.shape),
                  pl.BlockSpec((ts, LANES), lambda i, j: (j, 0))],
        out_specs=[pl.BlockSpec((None, N_HEADS, None, HEAD_W, ts), lambda i, j: (i, 0, j, 0, 0)),
                   pl.BlockSpec((None, N_HEADS, ts, V_HEAD_DIM + LANES),
                                lambda i, j: (i, 0, j, 0))],
        out_shape=[jax.ShapeDtypeStruct((b, N_HEADS, s // ts, HEAD_W, ts), BF16),
                   jax.ShapeDtypeStruct((b, N_HEADS, s, V_HEAD_DIM + LANES), BF16)],
        compiler_params=_params(),
        name="shared_latent_kv",
    )(h, gin, wdkv_x, gkv, wk_t, wv, rope_t)


def _q_kernel(h_ref, g_ref, wdq_ref, gq_ref, wuq_ref, rope_ref, q_ref, *, qscale):
    xn = _rms(h_ref[...], g_ref[...]).astype(BF16)
    qn = _rms(_dot(xn, wdq_ref[...]), gq_ref[...]).astype(BF16)
    q = _dot(qn, wuq_ref[...])
    rope = rope_ref[...] * qscale
    for hh in range(N_HEADS):
        base = hh * HEAD_W
        q_ref[hh, :, :QK_NOPE_DIM] = (q[:, base:base + QK_NOPE_DIM] * qscale).astype(BF16)
        q_ref[hh, :, QK_NOPE_DIM:] = (q[:, base + QK_NOPE_DIM:base + HEAD_W] * rope).astype(BF16)


def _q_proj(h, g, wdq, gq, wuq_p, rope_t, qscale):
    b, s, d = h.shape
    ts = SEQ_TILE
    return pl.pallas_call(
        functools.partial(_q_kernel, qscale=qscale),
        grid=(b, s // ts),
        in_specs=[pl.BlockSpec((None, ts, d), lambda i, j: (i, j, 0)),
                  _const_spec(g.shape), _const_spec(wdq.shape), _const_spec(gq.shape),
                  _const_spec(wuq_p.shape),
                  pl.BlockSpec((ts, LANES), lambda i, j: (j, 0))],
        out_specs=pl.BlockSpec((None, N_HEADS, ts, HEAD_W), lambda i, j: (i, 0, j, 0)),
        out_shape=jax.ShapeDtypeStruct((b, N_HEADS, s, HEAD_W), BF16),
        compiler_params=_params(),
        name="mla_q_proj",
    )(h, g, wdq, gq, wuq_p, rope_t)


def _attn_kernel(q_ref, kt_ref, v_ref, o_ref, sa_sc, sb_sc, m_sc, acc_sc):
    tq = Q_TILE
    tk = KV_TILE
    nq = q_ref.shape[0] // tq
    row = lax.broadcasted_iota(jnp.int32, (tq, tk), 0)
    col = lax.broadcasted_iota(jnp.int32, (tq, tk), 1)
    causal = col <= row

    def scores(pair, s_ref):
        i, j = pair
        s = _dot(q_ref[pl.ds(i * tq, tq), :], kt_ref[j])
        s_ref[...] = jnp.where(causal, s, MASK_VALUE) if i == j else s

    def softmax_pv(pair, s_ref):
        i, j = pair
        cols = [s_ref[:, c * LANES:(c + 1) * LANES] for c in range(tk // LANES)]
        m_cur = jnp.max(functools.reduce(jnp.maximum, cols), axis=-1, keepdims=True)
        if i == j:
            m_new = jnp.broadcast_to(m_cur, (tq, LANES))
        else:
            m_old = m_sc[i]
            m_new = jnp.maximum(m_old, m_cur)
            alpha = jnp.exp2(m_old - m_new)
        p = jnp.concatenate([jnp.exp2(c - m_new) for c in cols], axis=1).astype(BF16)
        pv = _dot(p, v_ref[pl.ds(j * tk, tk), :])
        if i == j:
            acc_sc[i] = pv
        else:
            acc_sc[i, :, :V_HEAD_DIM] = alpha * acc_sc[i, :, :V_HEAD_DIM] + pv[:, :V_HEAD_DIM]
            acc_sc[i, :, V_HEAD_DIM:] = alpha * acc_sc[i, :, V_HEAD_DIM:] + pv[:, V_HEAD_DIM:]
        m_sc[i] = m_new

    pairs = [(i, i) for i in range(nq)] + [(i, j) for i in range(1, nq) for j in range(i)]
    bufs = (sa_sc, sb_sc)
    scores(pairs[0], bufs[0])
    for t, pair in enumerate(pairs):
        if t + 1 < len(pairs):
            scores(pairs[t + 1], bufs[(t + 1) % 2])
        softmax_pv(pair, bufs[t % 2])
    for i in range(nq):
        o_ref[pl.ds(i * tq, tq), :] = (
            acc_sc[i, :, :V_HEAD_DIM] / acc_sc[i, :, V_HEAD_DIM:]).astype(o_ref.dtype)


def _attention(q, kt, v):
    b, nh, s, w = q.shape
    tq = Q_TILE
    nq = s // tq
    vw = v.shape[-1]
    return pl.pallas_call(
        _attn_kernel,
        grid=(b, nh),
        in_specs=[pl.BlockSpec((None, None, s, w), lambda i, h: (i, h, 0, 0)),
                  pl.BlockSpec((None, None) + kt.shape[2:], lambda i, h: (i, h, 0, 0, 0)),
                  pl.BlockSpec((None, None, s, vw), lambda i, h: (i, h, 0, 0))],
        out_specs=pl.BlockSpec((None, s, V_HEAD_DIM), lambda i, h: (i, 0, h)),
        out_shape=jax.ShapeDtypeStruct((b, s, nh * V_HEAD_DIM), BF16),
        scratch_shapes=[pltpu.VMEM((tq, KV_TILE), F32), pltpu.VMEM((tq, KV_TILE), F32),
                        pltpu.VMEM((nq, tq, LANES), F32), pltpu.VMEM((nq, tq, vw), F32)],
        compiler_params=pltpu.CompilerParams(
            dimension_semantics=("arbitrary", "arbitrary"),
            vmem_limit_bytes=VMEM_LIMIT_BYTES),
        name="mla_flash_attention",
    )(q, kt, v)


def _rope_table(seq):
    inv = 1.0 / (ROPE_THETA ** (jnp.arange(0, QK_ROPE_DIM, 2, dtype=F32) / QK_ROPE_DIM))
    ang = jnp.arange(seq, dtype=F32)[:, None] * inv[None, :]
    cos, sin = jnp.cos(ang), jnp.sin(ang)
    return jnp.concatenate([cos, cos, -sin, sin], axis=-1)


def _swap_halves(w):
    half = w.shape[-1] // 2
    return jnp.concatenate([w[..., half:], w[..., :half]], axis=-1)


def kernel(x, a_mix_norm, a_w_in, a_conv, a_w_out, b_mix_norm, b_w_dq, b_q_norm, b_w_uq, b_w_o,
           kv_in_norm, w_dkv, kv_norm, w_ukv, ffn_norm, ffn_w_up, ffn_conv, ffn_w_down, final_norm):
    seq = x.shape[1]
    rope_t = _rope_table(seq)
    qscale = (QK_NOPE_DIM + QK_ROPE_DIM) ** -0.5 * math.log2(math.e)
    half_gate = jnp.concatenate([jnp.full((1, D_FF), 0.5, F32), jnp.ones((1, D_FF), F32)], axis=1)

    def row(v):
        return v.reshape(1, -1)

    h = x
    kk = vv = None
    for layer in range(N_A_LAYERS + N_B_LAYERS):
        o = wo = None
        if layer < N_A_LAYERS:
            h = _mixer(h, row(a_mix_norm[layer]), a_w_in[layer].astype(BF16), a_conv[layer],
                       a_w_out[layer].astype(BF16))
        else:
            j = layer - N_A_LAYERS
            wuq = b_w_uq[j].reshape(Q_LORA_RANK, N_HEADS, QK_NOPE_DIM + QK_ROPE_DIM)
            wuq_rope = wuq[..., QK_NOPE_DIM:]
            wuq_p = jnp.concatenate([wuq[..., :QK_NOPE_DIM], wuq_rope, _swap_halves(wuq_rope)],
                                    axis=-1).reshape(Q_LORA_RANK, N_HEADS * HEAD_W).astype(BF16)
            q = _q_proj(h, row(b_mix_norm[j]), b_w_dq[j].astype(BF16), row(b_q_norm[j]), wuq_p,
                        rope_t, qscale)
            o = _attention(q, kk, vv)
            wo = b_w_o[j].astype(BF16)
        last = layer == N_A_LAYERS + N_B_LAYERS - 1
        h = _ffn(h, row(ffn_norm[layer]), ffn_w_up[layer].astype(BF16),
                 ffn_conv[layer] * half_gate, ffn_w_down[layer].astype(BF16), o=o, wo=wo,
                 gf=row(final_norm) if last else None)
        if layer == N_A_LAYERS - 1:
            k_rope_w = w_dkv[:, KV_LORA_RANK:]
            wdkv_x = jnp.concatenate([w_dkv, _swap_halves(k_rope_w)], axis=-1).astype(BF16)
            wukv = w_ukv.reshape(KV_LORA_RANK, N_HEADS, QK_NOPE_DIM + V_HEAD_DIM)
            wk_t = wukv[..., :QK_NOPE_DIM].reshape(KV_LORA_RANK, -1).T.astype(BF16)
            wv = wukv[..., QK_NOPE_DIM:].reshape(KV_LORA_RANK, -1).astype(BF16)
            kk, vv = _shared_kv(h, row(kv_in_norm), wdkv_x, row(kv_norm), wk_t, wv, rope_t)
    return h
```

```python
import functools
import math

import jax
import jax.numpy as jnp
from jax import lax
from jax.experimental import pallas as pl
from jax.experimental.pallas import tpu as pltpu

D_MODEL = 1024
N_A_LAYERS = 2
N_B_LAYERS = 2
CONV_WIDTH = 3
D_FF = 2816
N_HEADS = 8
QK_NOPE_DIM = 128
QK_ROPE_DIM = 64
V_HEAD_DIM = 128
Q_LORA_RANK = 512
KV_LORA_RANK = 256
ROPE_THETA = 10000.0
RMS_EPS = 1e-6

LANES = 128
SUBLANES = 8
MXU_DIM = 256
VMEM_LIMIT_BYTES = 56 * 1024 * 1024

SEQ_TILE = 512
HALO = SUBLANES
A_CHUNK = MXU_DIM
A_NCHUNK = D_MODEL // A_CHUNK
F_CHUNK = MXU_DIM
F_NCHUNK = D_FF // F_CHUNK
HEAD_W = QK_NOPE_DIM + 2 * QK_ROPE_DIM
Q_TILE = 512
KV_TILE = 512
MASK_VALUE = -1e30

BF16 = jnp.bfloat16
F32 = jnp.float32


def _rms(x, g):
    ms = jnp.mean(x * x, axis=-1, keepdims=True)
    return x * lax.rsqrt(ms + RMS_EPS) * g


def _dot(a, b):
    return jnp.dot(a, b, preferred_element_type=F32)


def _causal_conv3(buf_ref, cw, rows):
    x0 = buf_ref[pl.ds(HALO, rows), :]
    x1 = buf_ref[pl.ds(HALO - 1, rows), :]
    x2 = buf_ref[pl.ds(HALO - 2, rows), :]
    return cw[2:3, :] * x0 + cw[1:2, :] * x1 + cw[0:1, :] * x2


def _const_spec(shape):
    n = len(shape)
    return pl.BlockSpec(shape, lambda *_: (0,) * n, pipeline_mode=pl.Buffered(1))


def _params():
    return pltpu.CompilerParams(
        dimension_semantics=("arbitrary", "arbitrary"),
        vmem_limit_bytes=VMEM_LIMIT_BYTES)


def _mixer_kernel(h_ref, g_ref, win_ref, cw_ref, wout_ref, o_ref,
                  xn_sc, ch_sc, halo_sc, m_sc):
    ts, d = h_ref.shape

    @pl.when(pl.program_id(1) == 0)
    def _():
        halo_sc[...] = jnp.zeros_like(halo_sc)

    xn_sc[...] = _rms(h_ref[...], g_ref[...]).astype(BF16)

    def proj(part, k):
        lo = part * d + k * A_CHUNK
        return _dot(xn_sc[...], win_ref[:, lo:lo + A_CHUNK])

    for k in range(A_NCHUNK):
        ch_sc[pl.ds(0, HALO), :] = halo_sc[k]
        ch_sc[pl.ds(HALO, ts), :] = proj(1, k) * proj(2, k)
        y = _causal_conv3(ch_sc, cw_ref[:, k * A_CHUNK:(k + 1) * A_CHUNK], ts)
        halo_sc[k] = ch_sc[pl.ds(ts, HALO), :]
        m_sc[k] = (proj(0, k) * y).astype(BF16)
    acc = _dot(m_sc[0], wout_ref[0:A_CHUNK, :])
    for k in range(1, A_NCHUNK):
        acc = acc + _dot(m_sc[k], wout_ref[k * A_CHUNK:(k + 1) * A_CHUNK, :])
    o_ref[...] = h_ref[...] + acc


def _mixer(h, g, win, cw, wout):
    b, s, d = h.shape
    ts = SEQ_TILE
    tok = pl.BlockSpec((None, ts, d), lambda i, j: (i, j, 0))
    return pl.pallas_call(
        _mixer_kernel,
        grid=(b, s // ts),
        in_specs=[tok, _const_spec(g.shape), _const_spec(win.shape),
                  _const_spec(cw.shape), _const_spec(wout.shape)],
        out_specs=tok,
        out_shape=jax.ShapeDtypeStruct(h.shape, F32),
        scratch_shapes=[
            pltpu.VMEM((ts, d), BF16),
            pltpu.VMEM((ts + HALO, A_CHUNK), F32),
            pltpu.VMEM((A_NCHUNK, HALO, A_CHUNK), F32),
            pltpu.VMEM((A_NCHUNK, ts, A_CHUNK), BF16),
        ],
        compiler_params=_params(),
        name="short_conv_mixer",
    )(h, g, win, cw, wout)


def _ffn_kernel(*refs, has_oproj, has_final):
    refs = list(refs)
    h_ref = refs.pop(0)
    if has_oproj:
        o_ref, wo_ref = refs.pop(0), refs.pop(0)
    g_ref, wup_ref, cw_ref, wdn_ref = refs[:4]
    refs = refs[4:]
    if has_final:
        gf_ref = refs.pop(0)
    out_ref, xn_sc, upa_sc, upb_sc, halo_sc, actf_sc, act_sc = refs
    ts = h_ref.shape[0]

    @pl.when(pl.program_id(1) == 0)
    def _():
        halo_sc[...] = jnp.zeros_like(halo_sc)

    h = h_ref[...]
    if has_oproj:
        h = h + _dot(o_ref[...], wo_ref[...])
    out_ref[...] = h
    xn_sc[...] = _rms(h, g_ref[...]).astype(BF16)

    nslab = 2 * F_CHUNK // LANES
    half = ts // 2

    def up_into(buf, k):
        for part in range(2):
            lo = part * D_FF + k * F_CHUNK
            res = _dot(xn_sc[...], wup_ref[:, lo:lo + F_CHUNK])
            for c in range(nslab // 2):
                slab = part * (nslab // 2) + c
                buf[slab, pl.ds(0, HALO), :] = halo_sc[k, slab]
                buf[slab, pl.ds(HALO, ts), :] = res[:, c * LANES:(c + 1) * LANES]

    def gate_from(buf, k):
        def conv(slab, parity):
            part, c = divmod(slab, nslab // 2)
            lo = part * D_FF + k * F_CHUNK + c * LANES
            w = cw_ref[:, lo:lo + LANES]
            x = [buf[slab, pl.ds(HALO + parity - d, half, stride=2), :] for d in range(3)]
            return w[2:3, :] * x[0] + w[1:2, :] * x[1] + w[0:1, :] * x[2]

        for c in range(nslab // 2):
            for parity in range(2):
                hg = conv(c, parity)
                u = conv(c + nslab // 2, parity)
                actf_sc[c, pl.ds(parity, half, stride=2), :] = (hg + hg * jnp.tanh(hg)) * u
        for c in range(nslab):
            halo_sc[k, c] = buf[c, pl.ds(ts, HALO), :]
        act_sc[k] = jnp.concatenate([actf_sc[c] for c in range(nslab // 2)], axis=1).astype(BF16)

    bufs = (upa_sc, upb_sc)
    up_into(bufs[0], 0)
    for k in range(F_NCHUNK):
        if k + 1 < F_NCHUNK:
            up_into(bufs[(k + 1) % 2], k + 1)
        gate_from(bufs[k % 2], k)

    acc = _dot(act_sc[0], wdn_ref[0:F_CHUNK, :])
    for k in range(1, F_NCHUNK):
        acc = acc + _dot(act_sc[k], wdn_ref[k * F_CHUNK:(k + 1) * F_CHUNK, :])
    out = out_ref[...] + acc
    if has_final:
        out = _rms(out, gf_ref[...])
    out_ref[...] = out


def _ffn(h, g, wup, cw, wdn, *, o=None, wo=None, gf=None):
    b, s, d = h.shape
    ts = SEQ_TILE
    tok = pl.BlockSpec((None, ts, d), lambda i, j: (i, j, 0))
    args, specs = [h], [tok]
    if o is not None:
        args += [o, wo]
        specs += [tok, _const_spec(wo.shape)]
    args += [g, wup, cw, wdn]
    specs += [_const_spec(g.shape), _const_spec(wup.shape), _const_spec(cw.shape),
              _const_spec(wdn.shape)]
    if gf is not None:
        args.append(gf)
        specs.append(_const_spec(gf.shape))
    nslab = 2 * F_CHUNK // LANES
    return pl.pallas_call(
        functools.partial(_ffn_kernel, has_oproj=o is not None, has_final=gf is not None),
        grid=(b, s // ts),
        in_specs=specs,
        out_specs=tok,
        out_shape=jax.ShapeDtypeStruct(h.shape, F32),
        scratch_shapes=[
            pltpu.VMEM((ts, d), BF16),
            pltpu.VMEM((nslab, ts + HALO, LANES), F32),
            pltpu.VMEM((nslab, ts + HALO, LANES), F32),
            pltpu.VMEM((F_NCHUNK, nslab, HALO, LANES), F32),
            pltpu.VMEM((nslab // 2, ts, LANES), F32),
            pltpu.VMEM((F_NCHUNK, ts, F_CHUNK), BF16),
        ],
        compiler_params=_params(),
        name="conv_ffn",
    )(*args)


def _kv_kernel(h_ref, gin_ref, wdkv_ref, gkv_ref, wkt_ref, wv_ref, rope_ref,
               knt_ref, krt_ref, v_ref):
    xn = _rms(h_ref[...], gin_ref[...]).astype(BF16)
    ckv = _dot(xn, wdkv_ref[...])
    cn = _rms(ckv[:, :KV_LORA_RANK], gkv_ref[...])
    v = _dot(cn.astype(BF16), wv_ref[...])
    kt = _dot(wkt_ref[...], cn.T.astype(BF16))
    r = ckv[:, KV_LORA_RANK:] * rope_ref[...]
    krt_ref[...] = (r + pltpu.roll(r, QK_ROPE_DIM, axis=1)).T.astype(BF16)
    for hh in range(N_HEADS):
        knt_ref[hh] = kt[hh * QK_NOPE_DIM:(hh + 1) * QK_NOPE_DIM, :].astype(BF16)
        v_ref[hh] = v[:, hh * V_HEAD_DIM:(hh + 1) * V_HEAD_DIM].astype(BF16)


def _shared_kv(h, gin, wdkv_x, gkv, wk_t, wv, rope_t):
    b, s, d = h.shape
    ts = KV_TILE
    rope_w = 2 * QK_ROPE_DIM
    return pl.pallas_call(
        _kv_kernel,
        grid=(b, s // ts),
        in_specs=[pl.BlockSpec((None, ts, d), lambda i, j: (i, j, 0)),
                  _const_spec(gin.shape), _const_spec(wdkv_x.shape), _const_spec(gkv.shape),
                  _const_spec(wk_t.shape), _const_spec(wv.shape),
                  pl.BlockSpec((ts, LANES), lambda i, j: (j, 0))],
        out_specs=[pl.BlockSpec((None, N_HEADS, None, QK_NOPE_DIM, ts),
                                lambda i, j: (i, 0, j, 0, 0)),
                   pl.BlockSpec((None, None, rope_w, ts), lambda i, j: (i, j, 0, 0)),
                   pl.BlockSpec((None, N_HEADS, ts, V_HEAD_DIM), lambda i, j: (i, 0, j, 0))],
        out_shape=[jax.ShapeDtypeStruct((b, N_HEADS, s // ts, QK_NOPE_DIM, ts), BF16),
                   jax.ShapeDtypeStruct((b, s // ts, rope_w, ts), BF16),
                   jax.ShapeDtypeStruct((b, N_HEADS, s, V_HEAD_DIM), BF16)],
        compiler_params=_params(),
        name="shared_latent_kv",
    )(h, gin, wdkv_x, gkv, wk_t, wv, rope_t)


def _q_kernel(h_ref, g_ref, wdq_ref, gq_ref, wuq_ref, rope_ref, q_ref, *, qscale):
    xn = _rms(h_ref[...], g_ref[...]).astype(BF16)
    qn = _rms(_dot(xn, wdq_ref[...]), gq_ref[...]).astype(BF16)
    q = _dot(qn, wuq_ref[...])
    rope = rope_ref[...] * qscale
    for hh in range(N_HEADS):
        base = hh * HEAD_W
        q_ref[hh, :, :QK_NOPE_DIM] = (q[:, base:base + QK_NOPE_DIM] * qscale).astype(BF16)
        q_ref[hh, :, QK_NOPE_DIM:] = (q[:, base + QK_NOPE_DIM:base + HEAD_W] * rope).astype(BF16)


def _q_proj(h, g, wdq, gq, wuq_p, rope_t, qscale):
    b, s, d = h.shape
    ts = SEQ_TILE
    return pl.pallas_call(
        functools.partial(_q_kernel, qscale=qscale),
        grid=(b, s // ts),
        in_specs=[pl.BlockSpec((None, ts, d), lambda i, j: (i, j, 0)),
                  _const_spec(g.shape), _const_spec(wdq.shape), _const_spec(gq.shape),
                  _const_spec(wuq_p.shape),
                  pl.BlockSpec((ts, LANES), lambda i, j: (j, 0))],
        out_specs=pl.BlockSpec((None, N_HEADS, ts, HEAD_W), lambda i, j: (i, 0, j, 0)),
        out_shape=jax.ShapeDtypeStruct((b, N_HEADS, s, HEAD_W), BF16),
        compiler_params=_params(),
        name="mla_q_proj",
    )(h, g, wdq, gq, wuq_p, rope_t)


def _attn_kernel(q_ref, knt_ref, krt_ref, v_ref, o_ref, sa_sc, sb_sc, m_sc, acc_sc):
    tq = Q_TILE
    tk = KV_TILE
    nq = q_ref.shape[0] // tq
    hb = tk // 2
    row = lax.broadcasted_iota(jnp.int32, (hb, hb), 0)
    col = lax.broadcasted_iota(jnp.int32, (hb, hb), 1)
    sub_causal = col <= row

    def kt(j, c0, ncols):
        return jnp.concatenate([knt_ref[j, :, c0:c0 + ncols], krt_ref[j, :, c0:c0 + ncols]],
                               axis=0)

    def v_ones(r0, nrows):
        return jnp.concatenate([v_ref[pl.ds(r0, nrows), :], jnp.ones((nrows, LANES), BF16)],
                               axis=1)

    def scores(pair, s_ref):
        i, j = pair
        if i != j:
            s_ref[...] = _dot(q_ref[pl.ds(i * tq, tq), :], kt(j, 0, tk))
            return
        left = _dot(q_ref[pl.ds(i * tq, tq), :], kt(j, 0, hb))
        s_ref[pl.ds(0, hb), :hb] = jnp.where(sub_causal, left[:hb], MASK_VALUE)
        s_ref[pl.ds(hb, hb), :hb] = left[hb:]
        low = _dot(q_ref[pl.ds(i * tq + hb, hb), :], kt(j, hb, hb))
        s_ref[pl.ds(hb, hb), hb:] = jnp.where(sub_causal, low, MASK_VALUE)

    def softmax_rows(s_ref, r0, nrows, ncols, m_old):
        cols = [s_ref[pl.ds(r0, nrows), c * LANES:(c + 1) * LANES] for c in range(ncols // LANES)]
        m_cur = jnp.max(functools.reduce(jnp.maximum, cols), axis=-1, keepdims=True)
        if m_old is None:
            m_new = jnp.broadcast_to(m_cur, (nrows, LANES))
        else:
            m_new = jnp.maximum(m_old, m_cur)
        p = jnp.concatenate([jnp.exp2(c - m_new) for c in cols], axis=1).astype(BF16)
        return m_new, p

    def softmax_pv(pair, s_ref):
        i, j = pair
        if i == j:
            m_top, p_top = softmax_rows(s_ref, 0, hb, hb, None)
            acc_sc[i, pl.ds(0, hb), :] = _dot(p_top, v_ones(j * tk, hb))
            m_sc[i, pl.ds(0, hb), :] = m_top
            m_low, p_low = softmax_rows(s_ref, hb, hb, tk, None)
            acc_sc[i, pl.ds(hb, hb), :] = _dot(p_low, v_ones(j * tk, tk))
            m_sc[i, pl.ds(hb, hb), :] = m_low
            return
        m_old = m_sc[i]
        m_new, p = softmax_rows(s_ref, 0, tq, tk, m_old)
        alpha = jnp.exp2(m_old - m_new)
        pv = _dot(p, v_ones(j * tk, tk))
        acc_sc[i, :, :V_HEAD_DIM] = alpha * acc_sc[i, :, :V_HEAD_DIM] + pv[:, :V_HEAD_DIM]
        acc_sc[i, :, V_HEAD_DIM:] = alpha * acc_sc[i, :, V_HEAD_DIM:] + pv[:, V_HEAD_DIM:]
        m_sc[i] = m_new

    pairs = [(i, i) for i in range(nq)] + [(i, j) for i in range(1, nq) for j in range(i)]
    bufs = (sa_sc, sb_sc)
    scores(pairs[0], bufs[0])
    for t, pair in enumerate(pairs):
        if t + 1 < len(pairs):
            scores(pairs[t + 1], bufs[(t + 1) % 2])
        softmax_pv(pair, bufs[t % 2])
    for i in range(nq):
        o_ref[pl.ds(i * tq, tq), :] = (
            acc_sc[i, :, :V_HEAD_DIM] / acc_sc[i, :, V_HEAD_DIM:]).astype(o_ref.dtype)


def _attention(q, knt, krt, v):
    b, nh, s, w = q.shape
    tq = Q_TILE
    nq = s // tq
    return pl.pallas_call(
        _attn_kernel,
        grid=(b, nh),
        in_specs=[pl.BlockSpec((None, None, s, w), lambda i, h: (i, h, 0, 0)),
                  pl.BlockSpec((None, None) + knt.shape[2:], lambda i, h: (i, h, 0, 0, 0)),
                  pl.BlockSpec((None,) + krt.shape[1:], lambda i, h: (i, 0, 0, 0)),
                  pl.BlockSpec((None, None, s, V_HEAD_DIM), lambda i, h: (i, h, 0, 0))],
        out_specs=pl.BlockSpec((None, s, V_HEAD_DIM), lambda i, h: (i, 0, h)),
        out_shape=jax.ShapeDtypeStruct((b, s, nh * V_HEAD_DIM), BF16),
        scratch_shapes=[pltpu.VMEM((tq, KV_TILE), F32), pltpu.VMEM((tq, KV_TILE), F32),
                        pltpu.VMEM((nq, tq, LANES), F32),
                        pltpu.VMEM((nq, tq, V_HEAD_DIM + LANES), F32)],
        compiler_params=pltpu.CompilerParams(
            dimension_semantics=("arbitrary", "arbitrary"),
            vmem_limit_bytes=VMEM_LIMIT_BYTES),
        name="mla_flash_attention",
    )(q, knt, krt, v)


def _rope_table(seq):
    inv = 1.0 / (ROPE_THETA ** (jnp.arange(0, QK_ROPE_DIM, 2, dtype=F32) / QK_ROPE_DIM))
    ang = jnp.arange(seq, dtype=F32)[:, None] * inv[None, :]
    cos, sin = jnp.cos(ang), jnp.sin(ang)
    return jnp.concatenate([cos, cos, -sin, sin], axis=-1)


def _swap_halves(w):
    half = w.shape[-1] // 2
    return jnp.concatenate([w[..., half:], w[..., :half]], axis=-1)


def kernel(x, a_mix_norm, a_w_in, a_conv, a_w_out, b_mix_norm, b_w_dq, b_q_norm, b_w_uq, b_w_o,
           kv_in_norm, w_dkv, kv_norm, w_ukv, ffn_norm, ffn_w_up, ffn_conv, ffn_w_down, final_norm):
    seq = x.shape[1]
    rope_t = _rope_table(seq)
    qscale = (QK_NOPE_DIM + QK_ROPE_DIM) ** -0.5 * math.log2(math.e)
    half_gate = jnp.concatenate([jnp.full((1, D_FF), 0.5, F32), jnp.ones((1, D_FF), F32)], axis=1)

    def row(v):
        return v.reshape(1, -1)

    h = x
    shared_kv = None
    for layer in range(N_A_LAYERS + N_B_LAYERS):
        o = wo = None
        if layer < N_A_LAYERS:
            h = _mixer(h, row(a_mix_norm[layer]), a_w_in[layer].astype(BF16), a_conv[layer],
                       a_w_out[layer].astype(BF16))
        else:
            j = layer - N_A_LAYERS
            wuq = b_w_uq[j].reshape(Q_LORA_RANK, N_HEADS, QK_NOPE_DIM + QK_ROPE_DIM)
            wuq_rope = wuq[..., QK_NOPE_DIM:]
            wuq_p = jnp.concatenate([wuq[..., :QK_NOPE_DIM], wuq_rope, _swap_halves(wuq_rope)],
                                    axis=-1).reshape(Q_LORA_RANK, N_HEADS * HEAD_W).astype(BF16)
            q = _q_proj(h, row(b_mix_norm[j]), b_w_dq[j].astype(BF16), row(b_q_norm[j]), wuq_p,
                        rope_t, qscale)
            o = _attention(q, *shared_kv)
            wo = b_w_o[j].astype(BF16)
        last = layer == N_A_LAYERS + N_B_LAYERS - 1
        h = _ffn(h, row(ffn_norm[layer]), ffn_w_up[layer].astype(BF16),
                 ffn_conv[layer] * half_gate, ffn_w_down[layer].astype(BF16), o=o, wo=wo,
                 gf=row(final_norm) if last else None)
        if layer == N_A_LAYERS - 1:
            k_rope_w = w_dkv[:, KV_LORA_RANK:]
            wdkv_x = jnp.concatenate([w_dkv, _swap_halves(k_rope_w)], axis=-1).astype(BF16)
            wukv = w_ukv.reshape(KV_LORA_RANK, N_HEADS, QK_NOPE_DIM + V_HEAD_DIM)
            wk_t = wukv[..., :QK_NOPE_DIM].reshape(KV_LORA_RANK, -1).T.astype(BF16)
            wv = wukv[..., QK_NOPE_DIM:].reshape(KV_LORA_RANK, -1).astype(BF16)
            shared_kv = _shared_kv(h, row(kv_in_norm), wdkv_x, row(kv_norm), wk_t, wv, rope_t)
    return h
```
